```python
import math
import jax, jax.numpy as jnp
from jax import lax
import numpy as np

D_MODEL = 2048
BATCH = 16
SEQ = 2048
DEPTH = 1

N_HEADS = 32
N_KV_HEADS = 4
HEAD_DIM = 64
GROUP = N_HEADS // N_KV_HEADS
WINDOW = 128
BLOCK = WINDOW
ROPE_THETA = 500000.0
ROT_DIM = HEAD_DIM // 4
ATTN_SCALE = 1.0 / math.sqrt(HEAD_DIM)
D_CONV = D_MODEL // 2
CONV_WIDTH = 31
PEER_HEADS = 8
N_KEYS = 128
N_EXPERTS = N_KEYS * N_KEYS
D_KEY = 256
HALF_KEY = D_KEY // 2
TOPK = 16
TOKEN_CHUNK = 128
LN_EPS = 1e-5
ALPHA = (2 * DEPTH) ** 0.25
BETA = (8 * DEPTH) ** -0.25
Q_W = N_HEADS * HEAD_DIM
KV_W = N_KV_HEADS * HEAD_DIM
IN_COLS = 2 * D_CONV + Q_W + 2 * KV_W + 2 * D_MODEL
SPLITS = [2 * D_CONV, 2 * D_CONV + Q_W, 2 * D_CONV + Q_W + KV_W,
          2 * D_CONV + Q_W + 2 * KV_W, 2 * D_CONV + Q_W + 2 * KV_W + D_MODEL]

kernel_name = 'hybrid_conformer_swa_sink_peer_block'


def _layer_norm(x, g, b):
    xf = x.astype(jnp.float32)
    mu = xf.mean(-1, keepdims=True)
    var = jnp.square(xf - mu).mean(-1, keepdims=True)
    return ((xf - mu) * lax.rsqrt(var + LN_EPS) * g + b).astype(x.dtype)


def _rope_tables(positions, dtype):
    inv_freq = ROPE_THETA ** (-jnp.arange(0, ROT_DIM, 2, dtype=jnp.float32) / ROT_DIM)
    ang = positions.astype(jnp.float32)[..., None] * inv_freq
    return jnp.cos(ang)[:, :, None, :].astype(dtype), jnp.sin(ang)[:, :, None, :].astype(dtype)


def _rope_partial(t, cos, sin):
    half = ROT_DIM // 2
    t1, t2, rest = t[..., :half], t[..., half:ROT_DIM], t[..., ROT_DIM:]
    return jnp.concatenate([t1 * cos - t2 * sin, t2 * cos + t1 * sin, rest], axis=-1)


def _sliding_window_attention(q, k, v, sinks):
    B, S = q.shape[0], q.shape[1]
    nb = S // BLOCK
    qb = q.reshape(B, nb, BLOCK, N_KV_HEADS, GROUP, HEAD_DIM).transpose(1, 0, 2, 3, 4, 5)

    def band(t):
        tb = t.reshape(B, nb, BLOCK, N_KV_HEADS, HEAD_DIM)
        prev = jnp.pad(tb[:, :-1], ((0, 0), (1, 0), (0, 0), (0, 0), (0, 0)))
        return jnp.concatenate([prev, tb], axis=2).transpose(1, 0, 2, 3, 4)

    kb, vb = band(k), band(v)
    sink_logits = sinks.astype(jnp.float32).reshape(N_KV_HEADS, GROUP)
    qi = jnp.arange(BLOCK)[:, None]
    sj = jnp.arange(2 * BLOCK)[None, :]
    in_window = (sj > qi) & (sj <= qi + BLOCK)
    neg = jnp.finfo(jnp.float32).min

    def one_block(args):
        n, qn, kn, vn = args
        s = jnp.einsum('bqkgd,bskd->bkgqs', qn, kn,
                       preferred_element_type=jnp.float32) * ATTN_SCALE
        valid = in_window & (sj >= BLOCK * (1 - n))
        s = jnp.where(valid, s, neg)
        sink = jnp.broadcast_to(sink_logits[None, :, :, None, None], s.shape[:-1] + (1,))
        p = jax.nn.softmax(jnp.concatenate([s, sink], axis=-1), axis=-1)[..., :-1]
        return jnp.einsum('bkgqs,bskd->bqkgd', p.astype(vn.dtype), vn)

    out = lax.map(one_block, (jnp.arange(nb), qb, kb, vb))
    return out.transpose(1, 0, 2, 3, 4, 5).reshape(B, S, Q_W)


def _mixer(x, cos, sin, w_in, b_in, conv_dw_w, conv_dw_b, conv_ln_g, conv_ln_b,
           w_conv_out, attn_sinks, w_mix_out):
    B, S, _ = x.shape
    z = x @ w_in + b_in
    glu_in, q, k, v, gate_a, gate_b = jnp.split(z, SPLITS, axis=-1)
    ga, gb = jnp.split(glu_in, 2, axis=-1)
    u = ga * jax.nn.sigmoid(gb)
    u = lax.conv_general_dilated(u, conv_dw_w.astype(u.dtype), (1,), [(CONV_WIDTH - 1, 0)],
                                 dimension_numbers=('NWC', 'WIO', 'NWC'),
                                 feature_group_count=D_CONV) + conv_dw_b
    u = jax.nn.silu(_layer_norm(u, conv_ln_g, conv_ln_b))
    y_conv = u @ w_conv_out
    q = _rope_partial(q.reshape(B, S, N_HEADS, HEAD_DIM), cos, sin)
    k = _rope_partial(k.reshape(B, S, N_KV_HEADS, HEAD_DIM), cos, sin)
    v = v.reshape(B, S, N_KV_HEADS, HEAD_DIM)
    y_attn = _sliding_window_attention(q, k, v, attn_sinks)
    m = jax.nn.sigmoid(gate_a) * y_conv + jax.nn.sigmoid(gate_b) * y_attn
    return m @ w_mix_out


def _peer_chunk(xc, w_peer_q, sub_keys_1, sub_keys_2, expert_u, expert_v):
    C = xc.shape[0]
    q = (xc @ w_peer_q).reshape(C, PEER_HEADS, 2, HALF_KEY)
    s1 = jnp.einsum('chd,nd->chn', q[:, :, 0], sub_keys_1, preferred_element_type=jnp.float32)
    s2 = jnp.einsum('chd,nd->chn', q[:, :, 1], sub_keys_2, preferred_element_type=jnp.float32)
    v1, i1 = lax.top_k(s1, TOPK)
    v2, i2 = lax.top_k(s2, TOPK)
    cand = (v1[..., :, None] + v2[..., None, :]).reshape(C, PEER_HEADS, TOPK * TOPK)
    cidx = (i1[..., :, None] * N_KEYS + i2[..., None, :]).reshape(C, PEER_HEADS, TOPK * TOPK)
    best, pos = lax.top_k(cand, TOPK)
    eidx = jnp.take_along_axis(cidx, pos, axis=-1).reshape(C, PEER_HEADS * TOPK)
    g = jax.nn.softmax(best, axis=-1).reshape(C, PEER_HEADS * TOPK)
    u = expert_u[eidx]
    h = jnp.einsum('cd,ced->ce', xc, u, preferred_element_type=jnp.float32)
    a = (jax.nn.gelu(h, approximate=False) * g).astype(xc.dtype)
    return jnp.einsum('ce,ced->cd', a, expert_v[eidx])


def _peer(x, w_peer_q, sub_keys_1, sub_keys_2, expert_u, expert_v):
    B, S, D = x.shape
    xt = x.reshape(-1, TOKEN_CHUNK, D)
    y = lax.map(lambda xc: _peer_chunk(xc, w_peer_q, sub_keys_1, sub_keys_2,
                                       expert_u, expert_v), xt)
    return y.reshape(B, S, D)


def setup_inputs(seed: int = 0) -> dict:
    key = jax.random.key(seed)
    ks = jax.random.split(key, 24)
    f32 = jnp.float32
    nrm = lambda k, shape, scale: jax.random.normal(k, shape, f32) * scale
    sd = D_MODEL ** -0.5
    w_in = jnp.concatenate([
        nrm(ks[1], (DEPTH, D_MODEL, 2 * D_CONV), sd),
        nrm(ks[2], (DEPTH, D_MODEL, Q_W), sd),
        nrm(ks[3], (DEPTH, D_MODEL, KV_W), sd),
        nrm(ks[4], (DEPTH, D_MODEL, KV_W), sd * BETA),
        nrm(ks[5], (DEPTH, D_MODEL, 2 * D_MODEL), sd)], axis=-1)
    return {
        'x': jax.random.normal(ks[0], (BATCH, SEQ, D_MODEL), f32),
        'positions': jnp.broadcast_to(jnp.arange(SEQ, dtype=jnp.int32)[None, :], (BATCH, SEQ)),
        'w_in': w_in,
        'b_in': nrm(ks[6], (DEPTH, IN_COLS), 0.02),
        'conv_dw_w': nrm(ks[7], (DEPTH, CONV_WIDTH, 1, D_CONV), CONV_WIDTH ** -0.5),
        'conv_dw_b': nrm(ks[8], (DEPTH, D_CONV), 0.02),
        'conv_ln_g': 1.0 + nrm(ks[9], (DEPTH, D_CONV), 0.02),
        'conv_ln_b': nrm(ks[10], (DEPTH, D_CONV), 0.02),
        'w_conv_out': nrm(ks[11], (DEPTH, D_CONV, D_MODEL), D_CONV ** -0.5 * BETA),
        'attn_sinks': nrm(ks[12], (DEPTH, N_HEADS), 1.0),
        'w_mix_out': nrm(ks[13], (DEPTH, D_MODEL, D_MODEL), sd * BETA),
        'ln1_g': 1.0 + nrm(ks[14], (DEPTH, D_MODEL), 0.02),
        'ln1_b': nrm(ks[15], (DEPTH, D_MODEL), 0.02),
        'w_peer_q': nrm(ks[16], (DEPTH, D_MODEL, PEER_HEADS * D_KEY), sd),
        'sub_keys_1': nrm(ks[17], (DEPTH, N_KEYS, HALF_KEY), HALF_KEY ** -0.5),
        'sub_keys_2': nrm(ks[18], (DEPTH, N_KEYS, HALF_KEY), HALF_KEY ** -0.5),
        'expert_u': nrm(ks[19], (DEPTH, N_EXPERTS, D_MODEL), sd),
        'expert_v': nrm(ks[20], (DEPTH, N_EXPERTS, D_MODEL), BETA),
        'ln2_g': 1.0 + nrm(ks[21], (DEPTH, D_MODEL), 0.02),
        'ln2_b': nrm(ks[22], (DEPTH, D_MODEL), 0.02),
    }


def reference(x, positions, w_in, b_in, conv_dw_w, conv_dw_b, conv_ln_g, conv_ln_b,
              w_conv_out, attn_sinks, w_mix_out, ln1_g, ln1_b, w_peer_q, sub_keys_1,
              sub_keys_2, expert_u, expert_v, ln2_g, ln2_b):
    cos, sin = _rope_tables(positions, x.dtype)
    for l in range(DEPTH):
        mix = _mixer(x, cos, sin, w_in[l], b_in[l], conv_dw_w[l], conv_dw_b[l],
                     conv_ln_g[l], conv_ln_b[l], w_conv_out[l], attn_sinks[l], w_mix_out[l])
        x = _layer_norm(ALPHA * x + mix, ln1_g[l], ln1_b[l])
        ffn = _peer(x, w_peer_q[l], sub_keys_1[l], sub_keys_2[l], expert_u[l], expert_v[l])
        x = _layer_norm(ALPHA * x + ffn, ln2_g[l], ln2_b[l])
    return x
```

```python
import functools
import math

import jax
import jax.numpy as jnp
from jax import lax
from jax.experimental import pallas as pl
from jax.experimental.pallas import tpu as pltpu

F32 = jnp.float32
BF16 = jnp.bfloat16

N_HEADS = 32
N_KV_HEADS = 4
GROUP = N_HEADS // N_KV_HEADS
HEAD_DIM = 64
WINDOW = 128
ROPE_THETA = 500000.0
ROT_DIM = HEAD_DIM // 4
ATTN_SCALE = 1.0 / math.sqrt(HEAD_DIM)
PEER_HEADS = 8
N_KEYS = 128
TOPK = 16
LN_EPS = 1e-5

LANES = 128
SUBLANES = 8
VMEM_LIMIT = 56 * 1024 * 1024

TM_PROJ = 1024
TN_PROJ = 256
TS_MIX = 256
HALO = 32
TT_TOPK = 256
TT_PEER = 128
N_SPLIT = 2


def _params(sem):
    return pltpu.CompilerParams(dimension_semantics=sem, vmem_limit_bytes=VMEM_LIMIT)


def _glu_kernel(x_ref, wa_ref, wb_ref, ba_ref, bb_ref, o_ref, xb_ref):
    @pl.when(pl.program_id(1) == 0)
    def _():
        xb_ref[...] = x_ref[...].astype(BF16)

    xb = xb_ref[...]
    za = jnp.dot(xb, wa_ref[...], preferred_element_type=F32) + ba_ref[...]
    zb = jnp.dot(xb, wb_ref[...], preferred_element_type=F32) + bb_ref[...]
    o_ref[...] = za * jax.nn.sigmoid(zb)


def _rope_block(z, c, s1, s2):
    half = ROT_DIM // 2
    return z * c + pltpu.roll(z, LANES - half, axis=1) * s1 + pltpu.roll(z, half, axis=1) * s2


def _qk_kernel(x_ref, w_ref, b_ref, c_ref, s1_ref, s2_ref, o_ref, xb_ref):
    @pl.when(pl.program_id(1) == 0)
    def _():
        xb_ref[...] = x_ref[...].astype(BF16)

    z = jnp.dot(xb_ref[...], w_ref[...], preferred_element_type=F32) + b_ref[...]
    c, s1, s2 = c_ref[...], s1_ref[...], s2_ref[...]
    for k in range(z.shape[1] // LANES):
        sl = slice(LANES * k, LANES * (k + 1))
        o_ref[:, sl] = _rope_block(z[:, sl], c, s1, s2).astype(o_ref.dtype)


def _gv_kernel(n_gate_tiles, x_ref, w_ref, b_ref, o_ref, xb_ref):
    j = pl.program_id(1)

    @pl.when(j == 0)
    def _():
        xb_ref[...] = x_ref[...].astype(BF16)

    z = jnp.dot(xb_ref[...], w_ref[...], preferred_element_type=F32) + b_ref[...]

    @pl.when(j < n_gate_tiles)
    def _():
        o_ref[...] = jax.nn.sigmoid(z).astype(o_ref.dtype)

    @pl.when(j >= n_gate_tiles)
    def _():
        o_ref[...] = z.astype(o_ref.dtype)


def _proj_specs(T, D, N, n_w, extra_specs=()):
    tm, tn = TM_PROJ, TN_PROJ
    grid = (T // tm, N // tn)
    in_specs = [pl.BlockSpec((tm, D), lambda i, j: (i, 0))]
    in_specs += [pl.BlockSpec((D, tn), lambda i, j: (0, j)) for _ in range(n_w)]
    in_specs += [pl.BlockSpec((1, tn), lambda i, j: (0, j)) for _ in range(n_w)]
    in_specs += list(extra_specs)
    out_spec = pl.BlockSpec((tm, tn), lambda i, j: (i, j))
    scratch = [pltpu.VMEM((tm, D), BF16)]
    return grid, in_specs, out_spec, scratch


def _glu(x2, wa, wb, ba, bb):
    T, D = x2.shape
    N = wa.shape[1]
    grid, in_specs, out_spec, scratch = _proj_specs(T, D, N, 2)
    return pl.pallas_call(
        _glu_kernel, out_shape=jax.ShapeDtypeStruct((T, N), F32), grid=grid,
        in_specs=in_specs, out_specs=out_spec, scratch_shapes=scratch,
        compiler_params=_params(("parallel", "arbitrary")), name="glu")(x2, wa, wb, ba, bb)


def _qk(x2, w, b, c, s1, s2):
    T, D = x2.shape
    N = w.shape[1]
    tab = pl.BlockSpec((TM_PROJ, LANES), lambda i, j: (i, 0))
    grid, in_specs, out_spec, scratch = _proj_specs(T, D, N, 1, (tab, tab, tab))
    return pl.pallas_call(
        _qk_kernel, out_shape=jax.ShapeDtypeStruct((T, N), BF16), grid=grid,
        in_specs=in_specs, out_specs=out_spec, scratch_shapes=scratch,
        compiler_params=_params(("parallel", "arbitrary")), name="qk")(x2, w, b, c, s1, s2)


def _gv(x2, w, b, n_gate_cols):
    T, D = x2.shape
    N = w.shape[1]
    grid, in_specs, out_spec, scratch = _proj_specs(T, D, N, 1)
    return pl.pallas_call(
        functools.partial(_gv_kernel, n_gate_cols // TN_PROJ),
        out_shape=jax.ShapeDtypeStruct((T, N), BF16), grid=grid,
        in_specs=in_specs, out_specs=out_spec, scratch_shapes=scratch,
        compiler_params=_params(("parallel", "arbitrary")), name="gv")(x2, w, b)


def _attn_kernel(sink_ref, q_ref, kc_ref, kp_ref, vc_ref, vp_ref, o_ref):
    n = pl.program_id(1)
    blk = WINDOW
    lane = lax.broadcasted_iota(jnp.int32, (blk, LANES), 1)
    lo = lane < HEAD_DIM
    lane2 = lax.broadcasted_iota(jnp.int32, (2 * blk, LANES), 1)
    lo2 = lane2 < HEAD_DIM
    qi = lax.broadcasted_iota(jnp.int32, (blk, 2 * blk), 0)
    sj = lax.broadcasted_iota(jnp.int32, (blk, 2 * blk), 1)
    valid = (sj > qi) & (sj <= qi + blk) & (sj >= blk * (1 - n))
    neg = jnp.finfo(F32).min
    kk = jnp.concatenate([kp_ref[...], kc_ref[...]], axis=0).astype(F32)
    vv = jnp.concatenate([vp_ref[...], vc_ref[...]], axis=0).astype(F32)
    heads_per_tile = LANES // HEAD_DIM
    tiles_per_group = GROUP // heads_per_tile
    for g in range(N_KV_HEADS):
        tsl = slice(LANES * (g // heads_per_tile), LANES * (g // heads_per_tile + 1))

        def both_halves(t):
            r = pltpu.roll(t, HEAD_DIM, axis=1)
            return (jnp.where(lo2, t, r) if g % heads_per_tile == 0 else jnp.where(lo2, r, t)).astype(BF16)

        kd = both_halves(kk[:, tsl])
        vd = both_halves(vv[:, tsl])
        qs = []
        for c in range(tiles_per_group):
            col = LANES * (tiles_per_group * g + c)
            qt = q_ref[:, col:col + LANES]
            zero = jnp.zeros_like(qt)
            qs.append(jnp.where(lo, qt, zero))
            qs.append(jnp.where(lo, zero, qt))
        q8 = jnp.concatenate(qs, axis=0)
        s = lax.dot_general(q8, kd, (((1,), (1,)), ((), ())), preferred_element_type=F32) * ATTN_SCALE
        s3 = jnp.where(valid[None], s.reshape(GROUP, blk, 2 * blk), neg)
        sink = sink_ref[GROUP * g:GROUP * (g + 1)]
        m = jnp.maximum(jnp.max(s3, axis=-1, keepdims=True), sink)
        p = jnp.exp(s3 - m)
        denom = jnp.sum(p, axis=-1, keepdims=True) + jnp.exp(sink - m)
        p = p / denom
        o = jnp.dot(p.reshape(GROUP * blk, 2 * blk).astype(BF16), vd, preferred_element_type=F32)
        for c in range(tiles_per_group):
            oa = o[blk * (2 * c):blk * (2 * c + 1)]
            ob = o[blk * (2 * c + 1):blk * (2 * c + 2)]
            col = LANES * (tiles_per_group * g + c)
            o_ref[:, col:col + LANES] = jnp.where(lo, oa, ob).astype(o_ref.dtype)


def _attention(qk, gv, sinks, B, S):
    T = B * S
    blk = WINDOW
    nb = S // blk
    qw = N_HEADS * HEAD_DIM
    kvw = N_KV_HEADS * HEAD_DIM
    k_col = qw // kvw
    v_col = (gv.shape[1] - kvw) // kvw
    row = lambda b, n: b * nb + n
    prev = lambda b, n: jnp.maximum(b * nb + n - 1, 0)
    in_specs = [
        pl.BlockSpec((N_HEADS, 1, 1), lambda b, n: (0, 0, 0)),
        pl.BlockSpec((blk, qw), lambda b, n: (row(b, n), 0)),
        pl.BlockSpec((blk, kvw), lambda b, n: (row(b, n), k_col)),
        pl.BlockSpec((blk, kvw), lambda b, n: (prev(b, n), k_col)),
        pl.BlockSpec((blk, kvw), lambda b, n: (row(b, n), v_col)),
        pl.BlockSpec((blk, kvw), lambda b, n: (prev(b, n), v_col)),
    ]
    return pl.pallas_call(
        _attn_kernel, out_shape=jax.ShapeDtypeStruct((T, qw), BF16), grid=(B, nb),
        in_specs=in_specs, out_specs=pl.BlockSpec((blk, qw), lambda b, n: (row(b, n), 0)),
        compiler_params=_params(("parallel", "arbitrary")), name="attn",
    )(sinks.astype(F32).reshape(N_HEADS, 1, 1), qk, qk, qk, gv, gv)


def _layer_norm(x, g, b):
    mu = jnp.mean(x, axis=-1, keepdims=True)
    xc = x - mu
    var = jnp.mean(xc * xc, axis=-1, keepdims=True)
    return xc * lax.rsqrt(var + LN_EPS) * g + b


def _mix_kernel(alpha, u_ref, uh_ref, cw_ref, cb_ref, cg_ref, cbeta_ref, wco_ref, sa_ref, sb_ref,
                ya_ref, wmo_ref, x_ref, g1_ref, b1_ref, x1_ref, x1b_ref, win_ref, conv_ref):
    s = pl.program_id(1)
    ts, dc = u_ref.shape
    width = cw_ref.shape[0]
    win_ref[0:HALO, :] = jnp.where(s == 0, 0.0, uh_ref[...])
    win_ref[HALO:HALO + ts, :] = u_ref[...]
    first = HALO - (width - 1)
    for c in range(dc // LANES):
        cs = slice(LANES * c, LANES * (c + 1))
        acc = jnp.broadcast_to(cb_ref[:, cs], (ts, LANES))
        for j in range(width):
            acc = acc + cw_ref[j:j + 1, cs] * win_ref[first + j:first + j + ts, cs]
        conv_ref[:, cs] = acc
    h = _layer_norm(conv_ref[...], cg_ref[...], cbeta_ref[...])
    h = h * jax.nn.sigmoid(h)
    yc = jnp.dot(h.astype(BF16), wco_ref[...], preferred_element_type=F32)
    m = sa_ref[...].astype(F32) * yc + sb_ref[...].astype(F32) * ya_ref[...].astype(F32)
    mix = jnp.dot(m.astype(BF16), wmo_ref[...], preferred_element_type=F32)
    x1 = _layer_norm(alpha * x_ref[...] + mix, g1_ref[...], b1_ref[...])
    x1_ref[...] = x1
    x1b_ref[...] = x1.astype(BF16)


def _mix(u, cw, cb, cg, cbeta, wco, gv, ya, wmo, x2, g1, b1, alpha, B, S):
    T, D = x2.shape
    dc = u.shape[1]
    ts = TS_MIX
    ns = S // ts
    row = lambda b, s: b * ns + s
    halo = lambda b, s: jnp.maximum((b * S + s * ts) // HALO - 1, 0)
    const = lambda shape: pl.BlockSpec(shape, lambda b, s: (0, 0))
    in_specs = [
        pl.BlockSpec((ts, dc), lambda b, s: (row(b, s), 0)),
        pl.BlockSpec((HALO, dc), lambda b, s: (halo(b, s), 0)),
        const(cw.shape), const((1, dc)), const((1, dc)), const((1, dc)),
        const(wco.shape),
        pl.BlockSpec((ts, D), lambda b, s: (row(b, s), 0)),
        pl.BlockSpec((ts, D), lambda b, s: (row(b, s), 1)),
        pl.BlockSpec((ts, D), lambda b, s: (row(b, s), 0)),
        const(wmo.shape),
        pl.BlockSpec((ts, D), lambda b, s: (row(b, s), 0)),
        const((1, D)), const((1, D)),
    ]
    out_spec = pl.BlockSpec((ts, D), lambda b, s: (row(b, s), 0))
    return pl.pallas_call(
        functools.partial(_mix_kernel, alpha),
        out_shape=(jax.ShapeDtypeStruct((T, D), F32), jax.ShapeDtypeStruct((T, D), BF16)),
        grid=(B, ns), in_specs=in_specs, out_specs=(out_spec, out_spec),
        scratch_shapes=[pltpu.VMEM((HALO + ts, dc), F32), pltpu.VMEM((ts, dc), F32)],
        compiler_params=_params(("parallel", "arbitrary")), name="mix",
    )(u, u, cw, cb, cg, cbeta, wco, gv, gv, ya, wmo, x2, g1, b1)


def _topk_axis0(s, k, payload=None):
    n = s.shape[0]
    iota = lax.broadcasted_iota(jnp.int32, s.shape, 0)
    vals, outs = [], []
    for _ in range(k):
        m = jnp.max(s, axis=0, keepdims=True)
        idx = jnp.min(jnp.where(s == m, iota, n), axis=0, keepdims=True)
        sel = iota == idx
        vals.append(m)
        if payload is None:
            outs.append(idx)
        else:
            outs.append(jnp.max(jnp.where(sel, payload, -1), axis=0, keepdims=True))
        s = jnp.where(sel, -jnp.inf, s)
    return jnp.concatenate(vals, axis=0), jnp.concatenate(outs, axis=0)


def _topk_kernel(per_split, x_ref, wq_ref, k1_ref, k2_ref, e_ref, g_ref, *idx_refs):
    q = jnp.dot(x_ref[...], wq_ref[...], preferred_element_type=F32).astype(BF16)
    nt = (((1,), (1,)), ((), ()))
    half = k1_ref.shape[1]
    for h in range(PEER_HEADS):
        q1 = q[:, 2 * half * h:2 * half * h + half]
        q2 = q[:, 2 * half * h + half:2 * half * (h + 1)]
        s1 = lax.dot_general(k1_ref[...], q1, nt, preferred_element_type=F32)
        s2 = lax.dot_general(k2_ref[...], q2, nt, preferred_element_type=F32)
        v1, i1 = _topk_axis0(s1, TOPK)
        v2, i2 = _topk_axis0(s2, TOPK)
        cand = jnp.concatenate([v1[r:r + 1] + v2 for r in range(TOPK)], axis=0)
        cidx = jnp.concatenate([i1[r:r + 1] * N_KEYS + i2 for r in range(TOPK)], axis=0)
        best, eidx = _topk_axis0(cand, TOPK, payload=cidx)
        w = jnp.exp(best - best[0:1])
        gate = w / jnp.sum(w, axis=0, keepdims=True)
        rows = slice(TOPK * h, TOPK * (h + 1))
        e_ref[rows, :] = eidx
        g_ref[rows, :] = gate
        for part, idx_ref in enumerate(idx_refs):
            idx_ref[rows, :] = jnp.clip(eidx - part * per_split, 0, per_split - 1)


def _topk(x1b, wq, k1, k2, per_split):
    T, D = x1b.shape
    tt = TT_TOPK
    npair = PEER_HEADS * TOPK
    const = lambda shape: pl.BlockSpec(shape, lambda i: (0, 0))
    out_spec = pl.BlockSpec((npair, tt), lambda i: (0, i))
    return pl.pallas_call(
        functools.partial(_topk_kernel, per_split),
        out_shape=(jax.ShapeDtypeStruct((npair, T), jnp.int32), jax.ShapeDtypeStruct((npair, T), F32))
        + tuple(jax.ShapeDtypeStruct((npair, T), jnp.int32) for _ in range(N_SPLIT)),
        grid=(T // tt,),
        in_specs=[pl.BlockSpec((tt, D), lambda i: (i, 0)), const(wq.shape), const(k1.shape), const(k2.shape)],
        out_specs=(out_spec, out_spec) + tuple(out_spec for _ in range(N_SPLIT)),
        compiler_params=_params(("parallel",)), name="topk",
    )(x1b, wq, k1, k2)


def _rows_to_sublanes(parts):
    sub = lax.broadcasted_iota(jnp.int32, (SUBLANES, LANES), 0)
    bits = SUBLANES.bit_length() - 1
    parts = [parts[int(format(i, f"0{bits}b")[::-1], 2)] for i in range(SUBLANES)]
    step = SUBLANES // 2
    while len(parts) > 1:
        low = (sub & step) == 0
        nxt = []
        for a, b in zip(parts[0::2], parts[1::2]):
            t = jnp.where(low, a, pltpu.roll(b, step, axis=0))
            u = jnp.where(low, pltpu.roll(a, SUBLANES - step, axis=0), b)
            nxt.append(t + u)
        parts = nxt
        step //= 2
    return parts[0]


def _hside_kernel(idx_ref, x_ref, u_ref, h_ref):
    npair, tt = h_ref.shape
    lane = lax.broadcasted_iota(jnp.int32, (SUBLANES, tt), 1)
    h_ref[...] = jnp.zeros(h_ref.shape, F32)

    def token(t, carry):
        xf = x_ref[t].astype(F32)
        hit = lane == t
        for grp in range(npair // SUBLANES):
            parts = []
            for r in range(SUBLANES):
                e = idx_ref[grp * SUBLANES + r, t]
                prod = xf * u_ref[e].astype(F32)
                parts.append(prod[0:SUBLANES] + prod[SUBLANES:2 * SUBLANES])
            col = jnp.sum(_rows_to_sublanes(parts), axis=1, keepdims=True)
            rows = slice(grp * SUBLANES, (grp + 1) * SUBLANES)
            h_ref[rows, :] = jnp.where(hit, col, h_ref[rows, :])
        return carry

    lax.fori_loop(0, tt, token, 0)


def _hside(idx, x1p, table):
    npair, T = idx.shape
    tt = TT_PEER
    return pl.pallas_call(
        _hside_kernel, out_shape=jax.ShapeDtypeStruct((npair, T), F32), grid=(T // tt,),
        in_specs=[pl.BlockSpec((npair, tt), lambda i: (0, i), memory_space=pltpu.SMEM),
                  pl.BlockSpec((tt,) + x1p.shape[1:], lambda i: (i, 0, 0)),
                  pl.BlockSpec(memory_space=pltpu.VMEM)],
        out_specs=pl.BlockSpec((npair, tt), lambda i: (0, i)),
        compiler_params=_params(("arbitrary",)), name="hside",
    )(idx, x1p, table)


def _act_kernel(per_split, e_ref, g_ref, *refs):
    h_refs, a_refs = refs[:N_SPLIT], refs[N_SPLIT:]
    e = e_ref[...]
    in_part = [(e >= k * per_split) & (e < (k + 1) * per_split) for k in range(N_SPLIT)]
    h = h_refs[0][...]
    for k in range(1, N_SPLIT):
        h = jnp.where(in_part[k], h_refs[k][...], h)
    a = 0.5 * h * (1.0 + lax.erf(h * math.sqrt(0.5))) * g_ref[...]
    for k in range(N_SPLIT):
        a_refs[k][...] = jnp.where(in_part[k], a, 0.0)


def _act(e, g, hs, per_split):
    npair, T = e.shape
    tt = 2048
    spec = pl.BlockSpec((npair, tt), lambda i: (0, i))
    return pl.pallas_call(
        functools.partial(_act_kernel, per_split),
        out_shape=tuple(jax.ShapeDtypeStruct((npair, T), F32) for _ in range(N_SPLIT)),
        grid=(T // tt,), in_specs=[spec] * (2 + N_SPLIT), out_specs=tuple([spec] * N_SPLIT),
        compiler_params=_params(("parallel",)), name="act",
    )(e, g, *hs)


def _yside_kernel(has_prev, idx_ref, a_ref, v_ref, *refs):
    y_ref = refs[-1]
    npair, tt = idx_ref.shape
    n_acc = 4

    def token(t, carry):
        accs = [jnp.zeros(v_ref.shape[1:], F32) for _ in range(n_acc)]
        for p in range(npair):
            accs[p % n_acc] = accs[p % n_acc] + a_ref[p, t] * v_ref[idx_ref[p, t]].astype(F32)
        y = (accs[0] + accs[1]) + (accs[2] + accs[3])
        if has_prev:
            y = y + refs[0][t]
        y_ref[t] = y
        return carry

    lax.fori_loop(0, tt, token, 0)


def _yside(idx, a, table, prev):
    npair, T = idx.shape
    tt = TT_PEER
    tile = pl.BlockSpec((tt,) + table.shape[1:], lambda i: (i, 0, 0))
    smem = pl.BlockSpec((npair, tt), lambda i: (0, i), memory_space=pltpu.SMEM)
    in_specs = [smem, smem, pl.BlockSpec(memory_space=pltpu.VMEM)]
    args = [idx, a, table]
    if prev is not None:
        in_specs.append(tile)
        args.append(prev)
    return pl.pallas_call(
        functools.partial(_yside_kernel, prev is not None),
        out_shape=jax.ShapeDtypeStruct((T,) + table.shape[1:], F32), grid=(T // tt,),
        in_specs=in_specs, out_specs=tile,
        compiler_params=_params(("arbitrary",)), name="yside",
    )(*args)


def _ln2_kernel(alpha, x_ref, y_ref, g_ref, b_ref, o_ref):
    o_ref[...] = _layer_norm(alpha * x_ref[...] + y_ref[...], g_ref[...], b_ref[...])


def _ln2(x1, y, g, b, alpha):
    T, D = x1.shape
    tm = 512
    tile = pl.BlockSpec((tm, D), lambda i: (i, 0))
    vec = pl.BlockSpec((1, D), lambda i: (0, 0))
    return pl.pallas_call(
        functools.partial(_ln2_kernel, alpha), out_shape=jax.ShapeDtypeStruct((T, D), F32),
        grid=(T // tm,), in_specs=[tile, tile, vec, vec], out_specs=tile,
        compiler_params=_params(("parallel",)), name="ln2")(x1, y, g, b)


def _rope_tables(positions):
    T = positions.size
    half = ROT_DIM // 2
    inv_freq = ROPE_THETA ** (-jnp.arange(0, ROT_DIM, 2, dtype=F32) / ROT_DIM)
    ang = positions.reshape(T, 1).astype(F32) * inv_freq
    cos, sin = jnp.cos(ang), jnp.sin(ang)
    rest = HEAD_DIM - ROT_DIM
    zeros_h = jnp.zeros((T, half), F32)
    c = jnp.concatenate([cos, cos, jnp.ones((T, rest), F32)], axis=-1)
    s1 = jnp.concatenate([-sin, zeros_h, jnp.zeros((T, rest), F32)], axis=-1)
    s2 = jnp.concatenate([zeros_h, sin, jnp.zeros((T, rest), F32)], axis=-1)
    rep = LANES // HEAD_DIM
    return tuple(jnp.tile(t, (1, rep)) for t in (c, s1, s2))


def kernel(x, positions, w_in, b_in, conv_dw_w, conv_dw_b, conv_ln_g, conv_ln_b, w_conv_out, attn_sinks, w_mix_out, ln1_g, ln1_b, w_peer_q, sub_keys_1, sub_keys_2, expert_u, expert_v, ln2_g, ln2_b):
    B, S, D = x.shape
    T = B * S
    depth = w_in.shape[0]
    alpha = (2 * depth) ** 0.25
    dc = conv_dw_w.shape[-1]
    qw = N_HEADS * HEAD_DIM
    kvw = N_KV_HEADS * HEAD_DIM
    n_exp = expert_u.shape[1]
    per_split = n_exp // N_SPLIT
    packed = (D // LANES, LANES)
    c, s1, s2 = _rope_tables(positions)
    row = lambda v: v.reshape(1, -1)

    x2 = x.reshape(T, D)
    for l in range(depth):
        w = w_in[l].astype(BF16)
        b = b_in[l]
        o_q = 2 * dc
        o_k, o_v, o_ga = o_q + qw, o_q + qw + kvw, o_q + qw + 2 * kvw
        u = _glu(x2, w[:, :dc], w[:, dc:2 * dc], row(b[:dc]), row(b[dc:2 * dc]))
        qk = _qk(x2, w[:, o_q:o_v], row(b[o_q:o_v]), c, s1, s2)
        gv = _gv(x2, jnp.concatenate([w[:, o_ga:], w[:, o_v:o_ga]], axis=1),
                 row(jnp.concatenate([b[o_ga:], b[o_v:o_ga]])), 2 * D)
        ya = _attention(qk, gv, attn_sinks[l], B, S)
        x1, x1b = _mix(u, conv_dw_w[l].reshape(-1, dc), row(conv_dw_b[l]), row(conv_ln_g[l]),
                       row(conv_ln_b[l]), w_conv_out[l].astype(BF16), gv, ya,
                       w_mix_out[l].astype(BF16), x2, row(ln1_g[l]), row(ln1_b[l]), alpha, B, S)
        outs = _topk(x1b, w_peer_q[l].astype(BF16), sub_keys_1[l].astype(BF16),
                     sub_keys_2[l].astype(BF16), per_split)
        e, g, idxs = outs[0], outs[1], outs[2:]
        x1p = x1b.reshape((T,) + packed)
        ut = expert_u[l].astype(BF16).reshape((N_SPLIT, per_split) + packed)
        vt = expert_v[l].astype(BF16).reshape((N_SPLIT, per_split) + packed)
        hs = [_hside(idxs[k], x1p, ut[k]) for k in range(N_SPLIT)]
        acts = _act(e, g, hs, per_split)
        y = None
        for k in range(N_SPLIT):
            y = _yside(idxs[k], acts[k], vt[k], y)
        x2 = _ln2(x1, y.reshape(T, D), row(ln2_g[l]), row(ln2_b[l]), alpha)
    return x2.reshape(B, S, D)
```

```python
import functools
import math

import jax
import jax.numpy as jnp
from jax import lax
from jax.experimental import pallas as pl
from jax.experimental.pallas import tpu as pltpu

F32 = jnp.float32
BF16 = jnp.bfloat16

N_HEADS = 32
N_KV_HEADS = 4
GROUP = N_HEADS // N_KV_HEADS
HEAD_DIM = 64
WINDOW = 128
ROPE_THETA = 500000.0
ROT_DIM = HEAD_DIM // 4
ATTN_SCALE = 1.0 / math.sqrt(HEAD_DIM)
PEER_HEADS = 8
N_KEYS = 128
TOPK = 16
LN_EPS = 1e-5

LANES = 128
SUBLANES = 8
VMEM_LIMIT = 56 * 1024 * 1024

TM_PROJ = 1024
TN_PROJ = 256
TS_MIX = 256
HALO = 32
TT_TOPK = 256
TT_PEER = 128
N_SPLIT = 2
PAIR_GROUP = 16
ITEM_UNROLL = 4


def _params(sem):
    return pltpu.CompilerParams(dimension_semantics=sem, vmem_limit_bytes=VMEM_LIMIT)


def _glu_kernel(x_ref, wa_ref, wb_ref, ba_ref, bb_ref, o_ref, xb_ref):
    @pl.when(pl.program_id(1) == 0)
    def _():
        xb_ref[...] = x_ref[...].astype(BF16)

    xb = xb_ref[...]
    za = jnp.dot(xb, wa_ref[...], preferred_element_type=F32) + ba_ref[...]
    zb = jnp.dot(xb, wb_ref[...], preferred_element_type=F32) + bb_ref[...]
    o_ref[...] = za * jax.nn.sigmoid(zb)


def _rope_block(z, c, s1, s2):
    half = ROT_DIM // 2
    return z * c + pltpu.roll(z, LANES - half, axis=1) * s1 + pltpu.roll(z, half, axis=1) * s2


def _qk_kernel(x_ref, w_ref, b_ref, c_ref, s1_ref, s2_ref, o_ref, xb_ref):
    @pl.when(pl.program_id(1) == 0)
    def _():
        xb_ref[...] = x_ref[...].astype(BF16)

    z = jnp.dot(xb_ref[...], w_ref[...], preferred_element_type=F32) + b_ref[...]
    c, s1, s2 = c_ref[...], s1_ref[...], s2_ref[...]
    for k in range(z.shape[1] // LANES):
        sl = slice(LANES * k, LANES * (k + 1))
        o_ref[:, sl] = _rope_block(z[:, sl], c, s1, s2).astype(o_ref.dtype)


def _gv_kernel(n_gate_tiles, x_ref, w_ref, b_ref, o_ref, xb_ref):
    j = pl.program_id(1)

    @pl.when(j == 0)
    def _():
        xb_ref[...] = x_ref[...].astype(BF16)

    z = jnp.dot(xb_ref[...], w_ref[...], preferred_element_type=F32) + b_ref[...]

    @pl.when(j < n_gate_tiles)
    def _():
        o_ref[...] = jax.nn.sigmoid(z).astype(o_ref.dtype)

    @pl.when(j >= n_gate_tiles)
    def _():
        o_ref[...] = z.astype(o_ref.dtype)


def _proj_specs(T, D, N, n_w, extra_specs=()):
    tm, tn = TM_PROJ, TN_PROJ
    grid = (T // tm, N // tn)
    in_specs = [pl.BlockSpec((tm, D), lambda i, j: (i, 0))]
    in_specs += [pl.BlockSpec((D, tn), lambda i, j: (0, j)) for _ in range(n_w)]
    in_specs += [pl.BlockSpec((1, tn), lambda i, j: (0, j)) for _ in range(n_w)]
    in_specs += list(extra_specs)
    out_spec = pl.BlockSpec((tm, tn), lambda i, j: (i, j))
    scratch = [pltpu.VMEM((tm, D), BF16)]
    return grid, in_specs, out_spec, scratch


def _glu(x2, wa, wb, ba, bb):
    T, D = x2.shape
    N = wa.shape[1]
    grid, in_specs, out_spec, scratch = _proj_specs(T, D, N, 2)
    return pl.pallas_call(
        _glu_kernel, out_shape=jax.ShapeDtypeStruct((T, N), F32), grid=grid,
        in_specs=in_specs, out_specs=out_spec, scratch_shapes=scratch,
        compiler_params=_params(("parallel", "arbitrary")), name="glu")(x2, wa, wb, ba, bb)


def _qk(x2, w, b, c, s1, s2):
    T, D = x2.shape
    N = w.shape[1]
    tab = pl.BlockSpec((TM_PROJ, LANES), lambda i, j: (i, 0))
    grid, in_specs, out_spec, scratch = _proj_specs(T, D, N, 1, (tab, tab, tab))
    return pl.pallas_call(
        _qk_kernel, out_shape=jax.ShapeDtypeStruct((T, N), BF16), grid=grid,
        in_specs=in_specs, out_specs=out_spec, scratch_shapes=scratch,
        compiler_params=_params(("parallel", "arbitrary")), name="qk")(x2, w, b, c, s1, s2)


def _gv(x2, w, b, n_gate_cols):
    T, D = x2.shape
    N = w.shape[1]
    grid, in_specs, out_spec, scratch = _proj_specs(T, D, N, 1)
    return pl.pallas_call(
        functools.partial(_gv_kernel, n_gate_cols // TN_PROJ),
        out_shape=jax.ShapeDtypeStruct((T, N), BF16), grid=grid,
        in_specs=in_specs, out_specs=out_spec, scratch_shapes=scratch,
        compiler_params=_params(("parallel", "arbitrary")), name="gv")(x2, w, b)


def _attn_kernel(sink_ref, q_ref, kc_ref, kp_ref, vc_ref, vp_ref, o_ref):
    n = pl.program_id(1)
    blk = WINDOW
    lane = lax.broadcasted_iota(jnp.int32, (blk, LANES), 1)
    lo = lane < HEAD_DIM
    lane2 = lax.broadcasted_iota(jnp.int32, (2 * blk, LANES), 1)
    lo2 = lane2 < HEAD_DIM
    qi = lax.broadcasted_iota(jnp.int32, (blk, 2 * blk), 0)
    sj = lax.broadcasted_iota(jnp.int32, (blk, 2 * blk), 1)
    valid = (sj > qi) & (sj <= qi + blk) & (sj >= blk * (1 - n))
    neg = jnp.finfo(F32).min
    kk = jnp.concatenate([kp_ref[...], kc_ref[...]], axis=0).astype(F32)
    vv = jnp.concatenate([vp_ref[...], vc_ref[...]], axis=0).astype(F32)
    heads_per_tile = LANES // HEAD_DIM
    tiles_per_group = GROUP // heads_per_tile
    for g in range(N_KV_HEADS):
        tsl = slice(LANES * (g // heads_per_tile), LANES * (g // heads_per_tile + 1))

        def both_halves(t):
            r = pltpu.roll(t, HEAD_DIM, axis=1)
            return (jnp.where(lo2, t, r) if g % heads_per_tile == 0 else jnp.where(lo2, r, t)).astype(BF16)

        kd = both_halves(kk[:, tsl])
        vd = both_halves(vv[:, tsl])
        qs = []
        for c in range(tiles_per_group):
            col = LANES * (tiles_per_group * g + c)
            qt = q_ref[:, col:col + LANES]
            zero = jnp.zeros_like(qt)
            qs.append(jnp.where(lo, qt, zero))
            qs.append(jnp.where(lo, zero, qt))
        q8 = jnp.concatenate(qs, axis=0)
        s = lax.dot_general(q8, kd, (((1,), (1,)), ((), ())), preferred_element_type=F32) * ATTN_SCALE
        s3 = jnp.where(valid[None], s.reshape(GROUP, blk, 2 * blk), neg)
        sink = sink_ref[GROUP * g:GROUP * (g + 1)]
        m = jnp.maximum(jnp.max(s3, axis=-1, keepdims=True), sink)
        p = jnp.exp(s3 - m)
        denom = jnp.sum(p, axis=-1, keepdims=True) + jnp.exp(sink - m)
        p = p / denom
        o = jnp.dot(p.reshape(GROUP * blk, 2 * blk).astype(BF16), vd, preferred_element_type=F32)
        for c in range(tiles_per_group):
            oa = o[blk * (2 * c):blk * (2 * c + 1)]
            ob = o[blk * (2 * c + 1):blk * (2 * c + 2)]
            col = LANES * (tiles_per_group * g + c)
            o_ref[:, col:col + LANES] = jnp.where(lo, oa, ob).astype(o_ref.dtype)


def _attention(qk, gv, sinks, B, S):
    T = B * S
    blk = WINDOW
    nb = S // blk
    qw = N_HEADS * HEAD_DIM
    kvw = N_KV_HEADS * HEAD_DIM
    k_col = qw // kvw
    v_col = (gv.shape[1] - kvw) // kvw
    row = lambda b, n: b * nb + n
    prev = lambda b, n: jnp.maximum(b * nb + n - 1, 0)
    in_specs = [
        pl.BlockSpec((N_HEADS, 1, 1), lambda b, n: (0, 0, 0)),
        pl.BlockSpec((blk, qw), lambda b, n: (row(b, n), 0)),
        pl.BlockSpec((blk, kvw), lambda b, n: (row(b, n), k_col)),
        pl.BlockSpec((blk, kvw), lambda b, n: (prev(b, n), k_col)),
        pl.BlockSpec((blk, kvw), lambda b, n: (row(b, n), v_col)),
        pl.BlockSpec((blk, kvw), lambda b, n: (prev(b, n), v_col)),
    ]
    return pl.pallas_call(
        _attn_kernel, out_shape=jax.ShapeDtypeStruct((T, qw), BF16), grid=(B, nb),
        in_specs=in_specs, out_specs=pl.BlockSpec((blk, qw), lambda b, n: (row(b, n), 0)),
        compiler_params=_params(("parallel", "arbitrary")), name="attn",
    )(sinks.astype(F32).reshape(N_HEADS, 1, 1), qk, qk, qk, gv, gv)


def _layer_norm(x, g, b):
    mu = jnp.mean(x, axis=-1, keepdims=True)
    xc = x - mu
    var = jnp.mean(xc * xc, axis=-1, keepdims=True)
    return xc * lax.rsqrt(var + LN_EPS) * g + b


def _mix_kernel(alpha, u_ref, uh_ref, cw_ref, cb_ref, cg_ref, cbeta_ref, wco_ref, sa_ref, sb_ref,
                ya_ref, wmo_ref, x_ref, g1_ref, b1_ref, x1_ref, x1b_ref, win_ref, conv_ref):
    s = pl.program_id(1)
    ts, dc = u_ref.shape
    width = cw_ref.shape[0]
    win_ref[0:HALO, :] = jnp.where(s == 0, 0.0, uh_ref[...])
    win_ref[HALO:HALO + ts, :] = u_ref[...]
    first = HALO - (width - 1)
    for c in range(dc // LANES):
        cs = slice(LANES * c, LANES * (c + 1))
        acc = jnp.broadcast_to(cb_ref[:, cs], (ts, LANES))
        for j in range(width):
            acc = acc + cw_ref[j:j + 1, cs] * win_ref[first + j:first + j + ts, cs]
        conv_ref[:, cs] = acc
    h = _layer_norm(conv_ref[...], cg_ref[...], cbeta_ref[...])
    h = h * jax.nn.sigmoid(h)
    yc = jnp.dot(h.astype(BF16), wco_ref[...], preferred_element_type=F32)
    m = sa_ref[...].astype(F32) * yc + sb_ref[...].astype(F32) * ya_ref[...].astype(F32)
    mix = jnp.dot(m.astype(BF16), wmo_ref[...], preferred_element_type=F32)
    x1 = _layer_norm(alpha * x_ref[...] + mix, g1_ref[...], b1_ref[...])
    x1_ref[...] = x1
    x1b_ref[...] = x1.astype(BF16)


def _mix(u, cw, cb, cg, cbeta, wco, gv, ya, wmo, x2, g1, b1, alpha, B, S):
    T, D = x2.shape
    dc = u.shape[1]
    ts = TS_MIX
    ns = S // ts
    row = lambda b, s: b * ns + s
    halo = lambda b, s: jnp.maximum((b * S + s * ts) // HALO - 1, 0)
    const = lambda shape: pl.BlockSpec(shape, lambda b, s: (0, 0))
    in_specs = [
        pl.BlockSpec((ts, dc), lambda b, s: (row(b, s), 0)),
        pl.BlockSpec((HALO, dc), lambda b, s: (halo(b, s), 0)),
        const(cw.shape), const((1, dc)), const((1, dc)), const((1, dc)),
        const(wco.shape),
        pl.BlockSpec((ts, D), lambda b, s: (row(b, s), 0)),
        pl.BlockSpec((ts, D), lambda b, s: (row(b, s), 1)),
        pl.BlockSpec((ts, D), lambda b, s: (row(b, s), 0)),
        const(wmo.shape),
        pl.BlockSpec((ts, D), lambda b, s: (row(b, s), 0)),
        const((1, D)), const((1, D)),
    ]
    out_spec = pl.BlockSpec((ts, D), lambda b, s: (row(b, s), 0))
    return pl.pallas_call(
        functools.partial(_mix_kernel, alpha),
        out_shape=(jax.ShapeDtypeStruct((T, D), F32), jax.ShapeDtypeStruct((T, D), BF16)),
        grid=(B, ns), in_specs=in_specs, out_specs=(out_spec, out_spec),
        scratch_shapes=[pltpu.VMEM((HALO + ts, dc), F32), pltpu.VMEM((ts, dc), F32)],
        compiler_params=_params(("parallel", "arbitrary")), name="mix",
    )(u, u, cw, cb, cg, cbeta, wco, gv, gv, ya, wmo, x2, g1, b1)


def _topk_axis0(s, k, payload=None):
    n = s.shape[0]
    iota = lax.broadcasted_iota(jnp.int32, s.shape, 0)
    vals, outs = [], []
    for _ in range(k):
        m = jnp.max(s, axis=0, keepdims=True)
        idx = jnp.min(jnp.where(s == m, iota, n), axis=0, keepdims=True)
        sel = iota == idx
        vals.append(m)
        if payload is None:
            outs.append(idx)
        else:
            outs.append(jnp.max(jnp.where(sel, payload, -1), axis=0, keepdims=True))
        s = jnp.where(sel, -jnp.inf, s)
    return jnp.concatenate(vals, axis=0), jnp.concatenate(outs, axis=0)


def _partner_rows(x, j):
    n = x.shape[0]
    if j >= SUBLANES:
        return jnp.concatenate([x[(b ^ 1) * j:((b ^ 1) + 1) * j] for b in range(n // j)], axis=0)
    sub = lax.broadcasted_iota(jnp.int32, (SUBLANES, x.shape[1]), 0)
    low = (sub & j) == 0
    out = []
    for r in range(n // SUBLANES):
        slab = x[SUBLANES * r:SUBLANES * (r + 1)]
        out.append(jnp.where(low, pltpu.roll(slab, SUBLANES - j, axis=0), pltpu.roll(slab, j, axis=0)))
    return jnp.concatenate(out, axis=0)


def _sort_axis0(key, val):
    n = key.shape[0]
    row = lax.broadcasted_iota(jnp.int32, key.shape, 0)
    k = 2
    while k <= n:
        j = k // 2
        while j >= 1:
            pk, pv = _partner_rows(key, j), _partner_rows(val, j)
            sign = 1 - 2 * (((row & j) != 0).astype(jnp.int32) ^ ((row & k) != 0).astype(jnp.int32))
            take = (pk - key) * sign < 0
            key = jnp.where(take, pk, key)
            val = jnp.where(take, pv, val)
            j //= 2
        k *= 2
    return key, val


def _topk_kernel(per_split, x_ref, wq_ref, k1_ref, k2_ref, e_ref, g_ref, bnd_ref, *idx_refs):
    q = jnp.dot(x_ref[...], wq_ref[...], preferred_element_type=F32).astype(BF16)
    nt = (((1,), (1,)), ((), ()))
    half = k1_ref.shape[1]
    es, gs = [], []
    for h in range(PEER_HEADS):
        q1 = q[:, 2 * half * h:2 * half * h + half]
        q2 = q[:, 2 * half * h + half:2 * half * (h + 1)]
        s1 = lax.dot_general(k1_ref[...], q1, nt, preferred_element_type=F32)
        s2 = lax.dot_general(k2_ref[...], q2, nt, preferred_element_type=F32)
        v1, i1 = _topk_axis0(s1, TOPK)
        v2, i2 = _topk_axis0(s2, TOPK)
        cand = jnp.concatenate([v1[r:r + 1] + v2 for r in range(TOPK)], axis=0)
        cidx = jnp.concatenate([i1[r:r + 1] * N_KEYS + i2 for r in range(TOPK)], axis=0)
        best, eidx = _topk_axis0(cand, TOPK, payload=cidx)
        w = jnp.exp(best - best[0:1])
        es.append(eidx)
        gs.append(w / jnp.sum(w, axis=0, keepdims=True))
    e, g = _sort_axis0(jnp.concatenate(es, axis=0), jnp.concatenate(gs, axis=0))
    e_ref[...] = e
    g_ref[...] = g
    shift = PAIR_GROUP.bit_length() - 1
    bounds = []
    for part, idx_ref in enumerate(idx_refs):
        idx_ref[...] = jnp.clip(e - part * per_split, 0, per_split - 1).T
        below = jnp.sum((e < part * per_split).astype(jnp.int32), axis=0, keepdims=True)
        upto = jnp.sum((e < (part + 1) * per_split).astype(jnp.int32), axis=0, keepdims=True)
        bounds += [below >> shift, (upto + (PAIR_GROUP - 1)) >> shift]
    pad = bnd_ref.shape[0] - len(bounds)
    bnd_ref[...] = jnp.concatenate(bounds + [jnp.zeros_like(bounds[0])] * pad, axis=0)


def _topk(x1b, wq, k1, k2, per_split):
    T, D = x1b.shape
    tt = TT_TOPK
    npair = PEER_HEADS * TOPK
    const = lambda shape: pl.BlockSpec(shape, lambda i: (0, 0))
    pair_major = pl.BlockSpec((npair, tt), lambda i: (0, i))
    token_major = pl.BlockSpec((tt, npair), lambda i: (i, 0))
    return pl.pallas_call(
        functools.partial(_topk_kernel, per_split),
        out_shape=(jax.ShapeDtypeStruct((npair, T), jnp.int32), jax.ShapeDtypeStruct((npair, T), F32),
                   jax.ShapeDtypeStruct((SUBLANES, T), jnp.int32))
        + tuple(jax.ShapeDtypeStruct((T, npair), jnp.int32) for _ in range(N_SPLIT)),
        grid=(T // tt,),
        in_specs=[pl.BlockSpec((tt, D), lambda i: (i, 0)), const(wq.shape), const(k1.shape), const(k2.shape)],
        out_specs=(pair_major, pair_major, pl.BlockSpec((SUBLANES, tt), lambda i: (0, i)))
        + tuple(token_major for _ in range(N_SPLIT)),
        compiler_params=_params(("parallel",)), name="topk",
    )(x1b, wq, k1, k2)


def _rows_to_sublanes(parts):
    sub = lax.broadcasted_iota(jnp.int32, (SUBLANES, LANES), 0)
    bits = SUBLANES.bit_length() - 1
    parts = [parts[int(format(i, f"0{bits}b")[::-1], 2)] for i in range(SUBLANES)]
    step = SUBLANES // 2
    while len(parts) > 1:
        low = (sub & step) == 0
        nxt = []
        for a, b in zip(parts[0::2], parts[1::2]):
            t = jnp.where(low, a, pltpu.roll(b, step, axis=0))
            u = jnp.where(low, pltpu.roll(a, SUBLANES - step, axis=0), b)
            nxt.append(t + u)
        parts = nxt
        step //= 2
    return parts[0]


def _build_items(part, bnd_ref, items_ref, tt, groups):
    def per_token(t, n):
        first, last = bnd_ref[SUBLANES * t + 2 * part], bnd_ref[SUBLANES * t + 2 * part + 1]
        for g in range(groups):
            items_ref[n + g] = t * groups + first + g
        return n + (last - first)

    n = lax.fori_loop(0, tt, per_token, 0)
    final = items_ref[jnp.maximum(n - 1, 0)]
    for q in range(ITEM_UNROLL - 1):
        items_ref[n + q] = final
    return lax.shift_right_logical(n + (ITEM_UNROLL - 1), ITEM_UNROLL.bit_length() - 1)


def _hside_kernel(part, bnd_ref, idx_ref, x_ref, u_ref, h_ref, items_ref, sums_ref):
    npair, tt = h_ref.shape
    groups = npair // PAIR_GROUP
    gshift = groups.bit_length() - 1
    nblk = _build_items(part, bnd_ref, items_ref, tt, groups)

    @pl.when(pl.program_id(0) == 0)
    def _():
        sums_ref[...] = jnp.zeros(sums_ref.shape, F32)

    def block(i, carry):
        for q in range(ITEM_UNROLL):
            it = items_ref[i * ITEM_UNROLL + q]
            t = lax.shift_right_logical(it, gshift)
            base = it * PAIR_GROUP
            xf = x_ref[t].astype(F32)
            for sub in range(PAIR_GROUP // SUBLANES):
                parts = []
                for r in range(SUBLANES):
                    e = idx_ref[base + sub * SUBLANES + r]
                    prod = xf * u_ref[e].astype(F32)
                    parts.append(prod[0:SUBLANES] + prod[SUBLANES:2 * SUBLANES])
                row0 = pl.multiple_of(base + sub * SUBLANES, SUBLANES)
                sums_ref[pl.ds(row0, SUBLANES), :] = _rows_to_sublanes(parts)
        return carry

    lax.fori_loop(0, nblk, block, 0)

    lane = lax.broadcasted_iota(jnp.int32, (npair, tt), 1)
    h_ref[...] = jnp.zeros(h_ref.shape, F32)

    def finish(t, carry):
        rows = sums_ref[pl.ds(pl.multiple_of(t * npair, npair), npair), :]
        h_ref[...] = jnp.where(lane == t, jnp.sum(rows, axis=1, keepdims=True), h_ref[...])
        return carry

    lax.fori_loop(0, tt, finish, 0, unroll=8)


def _peer_specs(npair, tt):
    bnd = pl.BlockSpec((tt * SUBLANES,), lambda i: (i,), memory_space=pltpu.SMEM)
    idx = pl.BlockSpec((tt * npair,), lambda i: (i,), memory_space=pltpu.SMEM)
    return bnd, idx


def _items_scratch(npair, tt):
    return pltpu.SMEM((tt * (npair // PAIR_GROUP) + PAIR_GROUP + ITEM_UNROLL,), jnp.int32)


def _hside(part, bnd, idx, x1p, table, npair):
    T = x1p.shape[0]
    tt = TT_PEER
    return pl.pallas_call(
        functools.partial(_hside_kernel, part),
        out_shape=jax.ShapeDtypeStruct((npair, T), F32), grid=(T // tt,),
        in_specs=[*_peer_specs(npair, tt),
                  pl.BlockSpec((tt,) + x1p.shape[1:], lambda i: (i, 0, 0)),
                  pl.BlockSpec(memory_space=pltpu.VMEM)],
        out_specs=pl.BlockSpec((npair, tt), lambda i: (0, i)),
        scratch_shapes=[_items_scratch(npair, tt), pltpu.VMEM((tt * npair, LANES), F32)],
        compiler_params=_params(("arbitrary",)), name="hside",
    )(bnd, idx, x1p, table)


def _act_kernel(per_split, e_ref, g_ref, *refs):
    h_refs, a_refs = refs[:N_SPLIT], refs[N_SPLIT:]
    e = e_ref[...]
    in_part = [(e >= k * per_split) & (e < (k + 1) * per_split) for k in range(N_SPLIT)]
    h = h_refs[0][...]
    for k in range(1, N_SPLIT):
        h = jnp.where(in_part[k], h_refs[k][...], h)
    a = 0.5 * h * (1.0 + lax.erf(h * math.sqrt(0.5))) * g_ref[...]
    for k in range(N_SPLIT):
        a_refs[k][...] = jnp.where(in_part[k], a, 0.0).T


def _act(e, g, hs, per_split):
    npair, T = e.shape
    tt = 1024
    spec = pl.BlockSpec((npair, tt), lambda i: (0, i))
    out_spec = pl.BlockSpec((tt, npair), lambda i: (i, 0))
    return pl.pallas_call(
        functools.partial(_act_kernel, per_split),
        out_shape=tuple(jax.ShapeDtypeStruct((T, npair), F32) for _ in range(N_SPLIT)),
        grid=(T // tt,), in_specs=[spec] * (2 + N_SPLIT), out_specs=tuple([out_spec] * N_SPLIT),
        compiler_params=_params(("parallel",)), name="act",
    )(e, g, *hs)


def _yside_kernel(part, has_prev, npair, bnd_ref, idx_ref, a_ref, v_ref, *refs):
    y_ref, items_ref, part_ref = refs[-3], refs[-2], refs[-1]
    tt = y_ref.shape[0]
    groups = npair // PAIR_GROUP
    gshift = groups.bit_length() - 1
    n_acc = 4
    nblk = _build_items(part, bnd_ref, items_ref, tt, groups)

    @pl.when(pl.program_id(0) == 0)
    def _():
        part_ref[...] = jnp.zeros(part_ref.shape, F32)

    def block(i, carry):
        for q in range(ITEM_UNROLL):
            it = items_ref[i * ITEM_UNROLL + q]
            t = lax.shift_right_logical(it, gshift)
            base = it * PAIR_GROUP
            accs = [None] * n_acc
            for r in range(PAIR_GROUP):
                term = a_ref[base + r] * v_ref[idx_ref[base + r]].astype(F32)
                accs[r % n_acc] = term if accs[r % n_acc] is None else accs[r % n_acc] + term
            part_ref[it] = (accs[0] + accs[1]) + (accs[2] + accs[3])
        return carry

    lax.fori_loop(0, nblk, block, 0)

    def finish(t, carry):
        first, last = bnd_ref[SUBLANES * t + 2 * part], bnd_ref[SUBLANES * t + 2 * part + 1]
        y = refs[0][t] if has_prev else jnp.zeros(v_ref.shape[1:], F32)
        for g in range(groups):
            y = y + jnp.where((g >= first) & (g < last), part_ref[t * groups + g], 0.0)
        y_ref[t] = y
        return carry

    lax.fori_loop(0, tt, finish, 0, unroll=4)


def _yside(part, bnd, idx, a, table, prev, npair):
    T = idx.shape[0] // npair
    tt = TT_PEER
    tile = pl.BlockSpec((tt,) + table.shape[1:], lambda i: (i, 0, 0))
    bnd_spec, idx_spec = _peer_specs(npair, tt)
    in_specs = [bnd_spec, idx_spec, idx_spec, pl.BlockSpec(memory_space=pltpu.VMEM)]
    args = [bnd, idx, a, table]
    if prev is not None:
        in_specs.append(tile)
        args.append(prev)
    return pl.pallas_call(
        functools.partial(_yside_kernel, part, prev is not None, npair),
        out_shape=jax.ShapeDtypeStruct((T,) + table.shape[1:], F32), grid=(T // tt,),
        in_specs=in_specs, out_specs=tile,
        scratch_shapes=[_items_scratch(npair, tt),
                        pltpu.VMEM((tt * (npair // PAIR_GROUP),) + table.shape[1:], F32)],
        compiler_params=_params(("arbitrary",)), name="yside",
    )(*args)


def _ln2_kernel(alpha, x_ref, y_ref, g_ref, b_ref, o_ref):
    o_ref[...] = _layer_norm(alpha * x_ref[...] + y_ref[...], g_ref[...], b_ref[...])


def _ln2(x1, y, g, b, alpha):
    T, D = x1.shape
    tm = 512
    tile = pl.BlockSpec((tm, D), lambda i: (i, 0))
    vec = pl.BlockSpec((1, D), lambda i: (0, 0))
    return pl.pallas_call(
        functools.partial(_ln2_kernel, alpha), out_shape=jax.ShapeDtypeStruct((T, D), F32),
        grid=(T // tm,), in_specs=[tile, tile, vec, vec], out_specs=tile,
        compiler_params=_params(("parallel",)), name="ln2")(x1, y, g, b)


def _rope_tables(positions):
    T = positions.size
    half = ROT_DIM // 2
    inv_freq = ROPE_THETA ** (-jnp.arange(0, ROT_DIM, 2, dtype=F32) / ROT_DIM)
    ang = positions.reshape(T, 1).astype(F32) * inv_freq
    cos, sin = jnp.cos(ang), jnp.sin(ang)
    rest = HEAD_DIM - ROT_DIM
    zeros_h = jnp.zeros((T, half), F32)
    c = jnp.concatenate([cos, cos, jnp.ones((T, rest), F32)], axis=-1)
    s1 = jnp.concatenate([-sin, zeros_h, jnp.zeros((T, rest), F32)], axis=-1)
    s2 = jnp.concatenate([zeros_h, sin, jnp.zeros((T, rest), F32)], axis=-1)
    rep = LANES // HEAD_DIM
    return tuple(jnp.tile(t, (1, rep)) for t in (c, s1, s2))


def kernel(x, positions, w_in, b_in, conv_dw_w, conv_dw_b, conv_ln_g, conv_ln_b, w_conv_out, attn_sinks, w_mix_out, ln1_g, ln1_b, w_peer_q, sub_keys_1, sub_keys_2, expert_u, expert_v, ln2_g, ln2_b):
    B, S, D = x.shape
    T = B * S
    depth = w_in.shape[0]
    alpha = (2 * depth) ** 0.25
    dc = conv_dw_w.shape[-1]
    qw = N_HEADS * HEAD_DIM
    kvw = N_KV_HEADS * HEAD_DIM
    n_exp = expert_u.shape[1]
    per_split = n_exp // N_SPLIT
    packed = (D // LANES, LANES)
    c, s1, s2 = _rope_tables(positions)
    row = lambda v: v.reshape(1, -1)

    x2 = x.reshape(T, D)
    for l in range(depth):
        w = w_in[l].astype(BF16)
        b = b_in[l]
        o_q = 2 * dc
        o_k, o_v, o_ga = o_q + qw, o_q + qw + kvw, o_q + qw + 2 * kvw
        u = _glu(x2, w[:, :dc], w[:, dc:2 * dc], row(b[:dc]), row(b[dc:2 * dc]))
        qk = _qk(x2, w[:, o_q:o_v], row(b[o_q:o_v]), c, s1, s2)
        gv = _gv(x2, jnp.concatenate([w[:, o_ga:], w[:, o_v:o_ga]], axis=1),
                 row(jnp.concatenate([b[o_ga:], b[o_v:o_ga]])), 2 * D)
        ya = _attention(qk, gv, attn_sinks[l], B, S)
        x1, x1b = _mix(u, conv_dw_w[l].reshape(-1, dc), row(conv_dw_b[l]), row(conv_ln_g[l]),
                       row(conv_ln_b[l]), w_conv_out[l].astype(BF16), gv, ya,
                       w_mix_out[l].astype(BF16), x2, row(ln1_g[l]), row(ln1_b[l]), alpha, B, S)
        outs = _topk(x1b, w_peer_q[l].astype(BF16), sub_keys_1[l].astype(BF16),
                     sub_keys_2[l].astype(BF16), per_split)
        e, g = outs[0], outs[1]
        npair = e.shape[0]
        bnd = outs[2].T.reshape(-1)
        idxs = [i.reshape(-1) for i in outs[3:]]
        x1p = x1b.reshape((T,) + packed)
        ut = expert_u[l].astype(BF16).reshape((N_SPLIT, per_split) + packed)
        vt = expert_v[l].astype(BF16).reshape((N_SPLIT, per_split) + packed)
        hs = [_hside(k, bnd, idxs[k], x1p, ut[k], npair) for k in range(N_SPLIT)]
        acts = _act(e, g, hs, per_split)
        y = None
        for k in range(N_SPLIT):
            y = _yside(k, bnd, idxs[k], acts[k].reshape(-1), vt[k], y, npair)
        x2 = _ln2(x1, y.reshape(T, D), row(ln2_g[l]), row(ln2_b[l]), alpha)
    return x2.reshape(B, S, D)
```

```python
import functools
import math

import jax
import jax.numpy as jnp
from jax import lax
from jax.experimental import pallas as pl
from jax.experimental.pallas import tpu as pltpu

F32 = jnp.float32
BF16 = jnp.bfloat16

N_HEADS = 32
N_KV_HEADS = 4
GROUP = N_HEADS // N_KV_HEADS
HEAD_DIM = 64
WINDOW = 128
ROPE_THETA = 500000.0
ROT_DIM = HEAD_DIM // 4
ATTN_SCALE = 1.0 / math.sqrt(HEAD_DIM)
PEER_HEADS = 8
N_KEYS = 128
TOPK = 16
LN_EPS = 1e-5

LANES = 128
SUBLANES = 8
VMEM_LIMIT = 56 * 1024 * 1024

TM_PROJ = 1024
TN_PROJ = 512
TS_MIX = 256
HALO = 32
TT_TOPK = 256
TT_PEER = 128
N_SPLIT = 2
PAIR_GROUP = 16
ITEM_UNROLL = 4


def _params(sem):
    return pltpu.CompilerParams(dimension_semantics=sem, vmem_limit_bytes=VMEM_LIMIT)


def _glu_kernel(x_ref, wa_ref, wb_ref, ba_ref, bb_ref, o_ref, xb_ref):
    @pl.when(pl.program_id(1) == 0)
    def _():
        xb_ref[...] = x_ref[...].astype(BF16)

    xb = xb_ref[...]
    za = jnp.dot(xb, wa_ref[...], preferred_element_type=F32) + ba_ref[...]
    zb = jnp.dot(xb, wb_ref[...], preferred_element_type=F32) + bb_ref[...]
    o_ref[...] = za * jax.nn.sigmoid(zb)


def _rope_block(z, c, s1, s2):
    half = ROT_DIM // 2
    return z * c + pltpu.roll(z, LANES - half, axis=1) * s1 + pltpu.roll(z, half, axis=1) * s2


def _qkv_kernel(n_rope_cols, x_ref, w_ref, b_ref, c_ref, s1_ref, s2_ref, o_ref, xb_ref):
    j = pl.program_id(1)

    @pl.when(j == 0)
    def _():
        xb_ref[...] = x_ref[...].astype(BF16)

    z = jnp.dot(xb_ref[...], w_ref[...], preferred_element_type=F32) + b_ref[...]
    c, s1, s2 = c_ref[...], s1_ref[...], s2_ref[...]
    tn = z.shape[1]
    for k in range(tn // LANES):
        sl = slice(LANES * k, LANES * (k + 1))
        rotary = j * tn + LANES * k < n_rope_cols
        o_ref[:, sl] = jnp.where(rotary, _rope_block(z[:, sl], c, s1, s2), z[:, sl]).astype(o_ref.dtype)


def _gates_kernel(x_ref, w_ref, b_ref, o_ref, xb_ref):
    @pl.when(pl.program_id(1) == 0)
    def _():
        xb_ref[...] = x_ref[...].astype(BF16)

    z = jnp.dot(xb_ref[...], w_ref[...], preferred_element_type=F32) + b_ref[...]
    o_ref[...] = jax.nn.sigmoid(z).astype(o_ref.dtype)


def _proj_specs(T, D, N, n_w, extra_specs=()):
    tm, tn = TM_PROJ, TN_PROJ
    assert T % tm == 0 and N % tn == 0, (T, N)
    grid = (T // tm, N // tn)
    in_specs = [pl.BlockSpec((tm, D), lambda i, j: (i, 0))]
    in_specs += [pl.BlockSpec((D, tn), lambda i, j: (0, j)) for _ in range(n_w)]
    in_specs += [pl.BlockSpec((1, tn), lambda i, j: (0, j)) for _ in range(n_w)]
    in_specs += list(extra_specs)
    out_spec = pl.BlockSpec((tm, tn), lambda i, j: (i, j))
    scratch = [pltpu.VMEM((tm, D), BF16)]
    return grid, in_specs, out_spec, scratch


def _glu(x2, wa, wb, ba, bb):
    T, D = x2.shape
    N = wa.shape[1]
    grid, in_specs, out_spec, scratch = _proj_specs(T, D, N, 2)
    return pl.pallas_call(
        _glu_kernel, out_shape=jax.ShapeDtypeStruct((T, N), F32), grid=grid,
        in_specs=in_specs, out_specs=out_spec, scratch_shapes=scratch,
        compiler_params=_params(("parallel", "arbitrary")), name="glu")(x2, wa, wb, ba, bb)


def _qkv(x2, w, b, c, s1, s2, n_rope_cols):
    T, D = x2.shape
    N = w.shape[1]
    tab = pl.BlockSpec((TM_PROJ, LANES), lambda i, j: (i, 0))
    grid, in_specs, out_spec, scratch = _proj_specs(T, D, N, 1, (tab, tab, tab))
    return pl.pallas_call(
        functools.partial(_qkv_kernel, n_rope_cols),
        out_shape=jax.ShapeDtypeStruct((T, N), BF16), grid=grid,
        in_specs=in_specs, out_specs=out_spec, scratch_shapes=scratch,
        compiler_params=_params(("parallel", "arbitrary")), name="qkv")(x2, w, b, c, s1, s2)


def _gates(x2, w, b):
    T, D = x2.shape
    N = w.shape[1]
    grid, in_specs, out_spec, scratch = _proj_specs(T, D, N, 1)
    return pl.pallas_call(
        _gates_kernel, out_shape=jax.ShapeDtypeStruct((T, N), BF16), grid=grid,
        in_specs=in_specs, out_specs=out_spec, scratch_shapes=scratch,
        compiler_params=_params(("parallel", "arbitrary")), name="gates")(x2, w, b)


def _attn_kernel(sink_ref, q_ref, kc_ref, kp_ref, vc_ref, vp_ref, o_ref):
    n = pl.program_id(1)
    blk = WINDOW
    lane = lax.broadcasted_iota(jnp.int32, (blk, LANES), 1)
    lo = lane < HEAD_DIM
    lane2 = lax.broadcasted_iota(jnp.int32, (2 * blk, LANES), 1)
    lo2 = lane2 < HEAD_DIM
    qi = lax.broadcasted_iota(jnp.int32, (blk, 2 * blk), 0)
    sj = lax.broadcasted_iota(jnp.int32, (blk, 2 * blk), 1)
    valid = (sj > qi) & (sj <= qi + blk) & (sj >= blk * (1 - n))
    neg = jnp.finfo(F32).min
    kk = jnp.concatenate([kp_ref[...], kc_ref[...]], axis=0).astype(F32)
    vv = jnp.concatenate([vp_ref[...], vc_ref[...]], axis=0).astype(F32)
    heads_per_tile = LANES // HEAD_DIM
    tiles_per_group = GROUP // heads_per_tile
    for g in range(N_KV_HEADS):
        tsl = slice(LANES * (g // heads_per_tile), LANES * (g // heads_per_tile + 1))

        def both_halves(t):
            r = pltpu.roll(t, HEAD_DIM, axis=1)
            return (jnp.where(lo2, t, r) if g % heads_per_tile == 0 else jnp.where(lo2, r, t)).astype(BF16)

        kd = both_halves(kk[:, tsl])
        vd = both_halves(vv[:, tsl])
        qs = []
        for c in range(tiles_per_group):
            col = LANES * (tiles_per_group * g + c)
            qt = q_ref[:, col:col + LANES]
            zero = jnp.zeros_like(qt)
            qs.append(jnp.where(lo, qt, zero))
            qs.append(jnp.where(lo, zero, qt))
        q8 = jnp.concatenate(qs, axis=0)
        s = lax.dot_general(q8, kd, (((1,), (1,)), ((), ())), preferred_element_type=F32) * ATTN_SCALE
        s3 = jnp.where(valid[None], s.reshape(GROUP, blk, 2 * blk), neg)
        sink = sink_ref[GROUP * g:GROUP * (g + 1)]
        m = jnp.maximum(jnp.max(s3, axis=-1, keepdims=True), sink)
        p = jnp.exp(s3 - m)
        denom = jnp.sum(p, axis=-1, keepdims=True) + jnp.exp(sink - m)
        p = p / denom
        o = jnp.dot(p.reshape(GROUP * blk, 2 * blk).astype(BF16), vd, preferred_element_type=F32)
        for c in range(tiles_per_group):
            oa = o[blk * (2 * c):blk * (2 * c + 1)]
            ob = o[blk * (2 * c + 1):blk * (2 * c + 2)]
            col = LANES * (tiles_per_group * g + c)
            o_ref[:, col:col + LANES] = jnp.where(lo, oa, ob).astype(o_ref.dtype)


def _attention(qkv, sinks, B, S):
    T = B * S
    blk = WINDOW
    nb = S // blk
    qw = N_HEADS * HEAD_DIM
    kvw = N_KV_HEADS * HEAD_DIM
    k_col = qw // kvw
    v_col = k_col + 1
    row = lambda b, n: b * nb + n
    prev = lambda b, n: jnp.maximum(b * nb + n - 1, 0)
    in_specs = [
        pl.BlockSpec((N_HEADS, 1, 1), lambda b, n: (0, 0, 0)),
        pl.BlockSpec((blk, qw), lambda b, n: (row(b, n), 0)),
        pl.BlockSpec((blk, kvw), lambda b, n: (row(b, n), k_col)),
        pl.BlockSpec((blk, kvw), lambda b, n: (prev(b, n), k_col)),
        pl.BlockSpec((blk, kvw), lambda b, n: (row(b, n), v_col)),
        pl.BlockSpec((blk, kvw), lambda b, n: (prev(b, n), v_col)),
    ]
    return pl.pallas_call(
        _attn_kernel, out_shape=jax.ShapeDtypeStruct((T, qw), BF16), grid=(B, nb),
        in_specs=in_specs, out_specs=pl.BlockSpec((blk, qw), lambda b, n: (row(b, n), 0)),
        compiler_params=_params(("parallel", "arbitrary")), name="attn",
    )(sinks.astype(F32).reshape(N_HEADS, 1, 1), qkv, qkv, qkv, qkv, qkv)


def _layer_norm(x, g, b):
    mu = jnp.mean(x, axis=-1, keepdims=True)
    xc = x - mu
    var = jnp.mean(xc * xc, axis=-1, keepdims=True)
    return xc * lax.rsqrt(var + LN_EPS) * g + b


def _mix_kernel(alpha, u_ref, uh_ref, cw_ref, cb_ref, cg_ref, cbeta_ref, wco_ref, sa_ref, sb_ref,
                ya_ref, wmo_ref, x_ref, g1_ref, b1_ref, x1_ref, x1b_ref, win_ref, conv_ref):
    s = pl.program_id(1)
    ts, dc = u_ref.shape
    width = cw_ref.shape[0]
    win_ref[0:HALO, :] = jnp.where(s == 0, 0.0, uh_ref[...])
    win_ref[HALO:HALO + ts, :] = u_ref[...]
    first = HALO - (width - 1)
    for c in range(dc // LANES):
        cs = slice(LANES * c, LANES * (c + 1))
        acc = jnp.broadcast_to(cb_ref[:, cs], (ts, LANES))
        for j in range(width):
            acc = acc + cw_ref[j:j + 1, cs] * win_ref[first + j:first + j + ts, cs]
        conv_ref[:, cs] = acc
    h = _layer_norm(conv_ref[...], cg_ref[...], cbeta_ref[...])
    h = h * jax.nn.sigmoid(h)
    yc = jnp.dot(h.astype(BF16), wco_ref[...], preferred_element_type=F32)
    m = sa_ref[...].astype(F32) * yc + sb_ref[...].astype(F32) * ya_ref[...].astype(F32)
    mix = jnp.dot(m.astype(BF16), wmo_ref[...], preferred_element_type=F32)
    x1 = _layer_norm(alpha * x_ref[...] + mix, g1_ref[...], b1_ref[...])
    x1_ref[...] = x1
    x1b_ref[...] = x1.astype(BF16)


def _mix(u, cw, cb, cg, cbeta, wco, gv, ya, wmo, x2, g1, b1, alpha, B, S):
    T, D = x2.shape
    dc = u.shape[1]
    ts = TS_MIX
    ns = S // ts
    row = lambda b, s: b * ns + s
    halo = lambda b, s: jnp.maximum((b * S + s * ts) // HALO - 1, 0)
    const = lambda shape: pl.BlockSpec(shape, lambda b, s: (0, 0))
    in_specs = [
        pl.BlockSpec((ts, dc), lambda b, s: (row(b, s), 0)),
        pl.BlockSpec((HALO, dc), lambda b, s: (halo(b, s), 0)),
        const(cw.shape), const((1, dc)), const((1, dc)), const((1, dc)),
        const(wco.shape),
        pl.BlockSpec((ts, D), lambda b, s: (row(b, s), 0)),
        pl.BlockSpec((ts, D), lambda b, s: (row(b, s), 1)),
        pl.BlockSpec((ts, D), lambda b, s: (row(b, s), 0)),
        const(wmo.shape),
        pl.BlockSpec((ts, D), lambda b, s: (row(b, s), 0)),
        const((1, D)), const((1, D)),
    ]
    out_spec = pl.BlockSpec((ts, D), lambda b, s: (row(b, s), 0))
    return pl.pallas_call(
        functools.partial(_mix_kernel, alpha),
        out_shape=(jax.ShapeDtypeStruct((T, D), F32), jax.ShapeDtypeStruct((T, D), BF16)),
        grid=(B, ns), in_specs=in_specs, out_specs=(out_spec, out_spec),
        scratch_shapes=[pltpu.VMEM((HALO + ts, dc), F32), pltpu.VMEM((ts, dc), F32)],
        compiler_params=_params(("parallel", "arbitrary")), name="mix",
    )(u, u, cw, cb, cg, cbeta, wco, gv, gv, ya, wmo, x2, g1, b1)


def _topk_axis0(s, k, payload=None, rank=None):
    iota = lax.broadcasted_iota(jnp.int32, s.shape, 0) if rank is None else rank
    big = jnp.iinfo(jnp.int32).max
    vals, outs = [], []
    for _ in range(k):
        m = jnp.max(s, axis=0, keepdims=True)
        idx = jnp.min(jnp.where(s == m, iota, big), axis=0, keepdims=True)
        sel = iota == idx
        vals.append(m)
        if payload is None:
            outs.append(idx)
        else:
            outs.append(jnp.max(jnp.where(sel, payload, -1), axis=0, keepdims=True))
        s = jnp.where(sel, -jnp.inf, s)
    return jnp.concatenate(vals, axis=0), jnp.concatenate(outs, axis=0)


def _candidate_grid(v1, i1, v2, i2):
    k, t = v1.shape
    sub = lax.broadcasted_iota(jnp.int32, (SUBLANES, t), 0)
    vals, ids, flats = [], [], []
    for a in range(k // 2):
        nb = k // (a + 1)
        for b0 in range(0, nb, SUBLANES):
            live = sub < (nb - b0)
            vals.append(jnp.where(live, v1[a:a + 1] + v2[b0:b0 + SUBLANES], -jnp.inf))
            ids.append(i1[a:a + 1] * N_KEYS + i2[b0:b0 + SUBLANES])
            flats.append(a * k + b0 + sub)
    for a0 in range(k // 2, k, SUBLANES):
        vals.append(v1[a0:a0 + SUBLANES] + v2[0:1])
        ids.append(i1[a0:a0 + SUBLANES] * N_KEYS + i2[0:1])
        flats.append((a0 + sub) * k)
    return jnp.concatenate(vals, axis=0), jnp.concatenate(ids, axis=0), jnp.concatenate(flats, axis=0)


def _partner_rows(x, j):
    n = x.shape[0]
    if j >= SUBLANES:
        return jnp.concatenate([x[(b ^ 1) * j:((b ^ 1) + 1) * j] for b in range(n // j)], axis=0)
    sub = lax.broadcasted_iota(jnp.int32, (SUBLANES, x.shape[1]), 0)
    low = (sub & j) == 0
    out = []
    for r in range(n // SUBLANES):
        slab = x[SUBLANES * r:SUBLANES * (r + 1)]
        out.append(jnp.where(low, pltpu.roll(slab, SUBLANES - j, axis=0), pltpu.roll(slab, j, axis=0)))
    return jnp.concatenate(out, axis=0)


def _sort_axis0(key, val):
    n = key.shape[0]
    row = lax.broadcasted_iota(jnp.int32, key.shape, 0)
    k = 2
    while k <= n:
        j = k // 2
        while j >= 1:
            pk, pv = _partner_rows(key, j), _partner_rows(val, j)
            sign = 1 - 2 * (((row & j) != 0).astype(jnp.int32) ^ ((row & k) != 0).astype(jnp.int32))
            take = (pk - key) * sign < 0
            key = jnp.where(take, pk, key)
            val = jnp.where(take, pv, val)
            j //= 2
        k *= 2
    return key, val


def _topk_kernel(per_split, x_ref, wq_ref, k1_ref, k2_ref, e_ref, g_ref, bnd_ref, *idx_refs):
    q = jnp.dot(x_ref[...], wq_ref[...], preferred_element_type=F32).astype(BF16)
    nt = (((1,), (1,)), ((), ()))
    half = k1_ref.shape[1]
    es, gs = [], []
    for h in range(PEER_HEADS):
        q1 = q[:, 2 * half * h:2 * half * h + half]
        q2 = q[:, 2 * half * h + half:2 * half * (h + 1)]
        s1 = lax.dot_general(k1_ref[...], q1, nt, preferred_element_type=F32)
        s2 = lax.dot_general(k2_ref[...], q2, nt, preferred_element_type=F32)
        v1, i1 = _topk_axis0(s1, TOPK)
        v2, i2 = _topk_axis0(s2, TOPK)
        cand, cidx, flat = _candidate_grid(v1, i1, v2, i2)
        best, eidx = _topk_axis0(cand, TOPK, payload=cidx, rank=flat)
        w = jnp.exp(best - best[0:1])
        es.append(eidx)
        gs.append(w / jnp.sum(w, axis=0, keepdims=True))
    e, g = _sort_axis0(jnp.concatenate(es, axis=0), jnp.concatenate(gs, axis=0))
    e_ref[...] = e
    g_ref[...] = g
    shift = PAIR_GROUP.bit_length() - 1
    bounds = []
    for part, idx_ref in enumerate(idx_refs):
        idx_ref[...] = jnp.clip(e - part * per_split, 0, per_split - 1).T
        below = jnp.sum((e < part * per_split).astype(jnp.int32), axis=0, keepdims=True)
        upto = jnp.sum((e < (part + 1) * per_split).astype(jnp.int32), axis=0, keepdims=True)
        bounds += [below >> shift, (upto + (PAIR_GROUP - 1)) >> shift]
    pad = bnd_ref.shape[0] - len(bounds)
    bnd_ref[...] = jnp.concatenate(bounds + [jnp.zeros_like(bounds[0])] * pad, axis=0)


def _topk(x1b, wq, k1, k2, per_split):
    T, D = x1b.shape
    tt = TT_TOPK
    npair = PEER_HEADS * TOPK
    const = lambda shape: pl.BlockSpec(shape, lambda i: (0, 0))
    pair_major = pl.BlockSpec((npair, tt), lambda i: (0, i))
    token_major = pl.BlockSpec((tt, npair), lambda i: (i, 0))
    return pl.pallas_call(
        functools.partial(_topk_kernel, per_split),
        out_shape=(jax.ShapeDtypeStruct((npair, T), jnp.int32), jax.ShapeDtypeStruct((npair, T), F32),
                   jax.ShapeDtypeStruct((SUBLANES, T), jnp.int32))
        + tuple(jax.ShapeDtypeStruct((T, npair), jnp.int32) for _ in range(N_SPLIT)),
        grid=(T // tt,),
        in_specs=[pl.BlockSpec((tt, D), lambda i: (i, 0)), const(wq.shape), const(k1.shape), const(k2.shape)],
        out_specs=(pair_major, pair_major, pl.BlockSpec((SUBLANES, tt), lambda i: (0, i)))
        + tuple(token_major for _ in range(N_SPLIT)),
        compiler_params=_params(("parallel",)), name="topk",
    )(x1b, wq, k1, k2)


def _build_items(part, bnd_ref, items_ref, tt, groups):
    def per_token(t, n):
        first, last = bnd_ref[SUBLANES * t + 2 * part], bnd_ref[SUBLANES * t + 2 * part + 1]
        for g in range(groups):
            items_ref[n + g] = t * groups + first + g
        return n + (last - first)

    n = lax.fori_loop(0, tt, per_token, 0)
    final = items_ref[jnp.maximum(n - 1, 0)]
    for q in range(ITEM_UNROLL - 1):
        items_ref[n + q] = final
    return lax.shift_right_logical(n + (ITEM_UNROLL - 1), ITEM_UNROLL.bit_length() - 1)


def _rows_to_sublanes(parts):
    sub = lax.broadcasted_iota(jnp.int32, (SUBLANES, LANES), 0)
    bits = SUBLANES.bit_length() - 1
    parts = [parts[int(format(i, f"0{bits}b")[::-1], 2)] for i in range(SUBLANES)]
    step = SUBLANES // 2
    while len(parts) > 1:
        low = (sub & step) == 0
        nxt = []
        for a, b in zip(parts[0::2], parts[1::2]):
            if 2 * step == SUBLANES:
                nxt.append(jnp.where(low, a, b) + pltpu.roll(jnp.where(low, b, a), step, axis=0))
            else:
                t = jnp.where(low, a, pltpu.roll(b, step, axis=0))
                u = jnp.where(low, pltpu.roll(a, SUBLANES - step, axis=0), b)
                nxt.append(t + u)
        parts = nxt
        step //= 2
    return parts[0]


def _hside_kernel(part, bnd_ref, idx_ref, x_ref, u_ref, h_ref, items_ref, sums_ref):
    npair, tt = h_ref.shape
    groups = npair // PAIR_GROUP
    gshift = groups.bit_length() - 1
    nblk = _build_items(part, bnd_ref, items_ref, tt, groups)

    @pl.when(pl.program_id(0) == 0)
    def _():
        sums_ref[...] = jnp.zeros(sums_ref.shape, F32)

    def block(i, carry):
        for q in range(ITEM_UNROLL):
            it = items_ref[i * ITEM_UNROLL + q]
            t = lax.shift_right_logical(it, gshift)
            base = it * PAIR_GROUP
            xf = x_ref[t].astype(F32)
            for sub in range(PAIR_GROUP // SUBLANES):
                parts = []
                for r in range(SUBLANES):
                    prod = xf * u_ref[idx_ref[base + sub * SUBLANES + r]].astype(F32)
                    parts.append(prod[0:SUBLANES] + prod[SUBLANES:2 * SUBLANES])
                row0 = pl.multiple_of(base + sub * SUBLANES, SUBLANES)
                sums_ref[pl.ds(row0, SUBLANES), :] = _rows_to_sublanes(parts)
        return carry

    lax.fori_loop(0, nblk, block, 0)

    lane = lax.broadcasted_iota(jnp.int32, (npair, tt), 1)
    h_ref[...] = jnp.zeros(h_ref.shape, F32)

    def finish(t, carry):
        rows = sums_ref[pl.ds(pl.multiple_of(t * npair, npair), npair), :]
        h_ref[...] = jnp.where(lane == t, jnp.sum(rows, axis=1, keepdims=True), h_ref[...])
        return carry

    lax.fori_loop(0, tt, finish, 0, unroll=8)


def _peer_specs(npair, tt):
    bnd = pl.BlockSpec((tt * SUBLANES,), lambda i: (i,), memory_space=pltpu.SMEM)
    idx = pl.BlockSpec((tt * npair,), lambda i: (i,), memory_space=pltpu.SMEM)
    return bnd, idx


def _items_scratch(npair, tt):
    return pltpu.SMEM((tt * (npair // PAIR_GROUP) + PAIR_GROUP + ITEM_UNROLL,), jnp.int32)


def _hside(part, bnd, idx, x1p, table, npair):
    T = x1p.shape[0]
    tt = TT_PEER
    return pl.pallas_call(
        functools.partial(_hside_kernel, part),
        out_shape=jax.ShapeDtypeStruct((npair, T), F32), grid=(T // tt,),
        in_specs=[*_peer_specs(npair, tt),
                  pl.BlockSpec((tt,) + x1p.shape[1:], lambda i: (i, 0, 0)),
                  pl.BlockSpec(memory_space=pltpu.VMEM)],
        out_specs=pl.BlockSpec((npair, tt), lambda i: (0, i)),
        scratch_shapes=[_items_scratch(npair, tt), pltpu.VMEM((tt * npair, LANES), F32)],
        compiler_params=_params(("arbitrary",)), name="hside",
    )(bnd, idx, x1p, table)


def _act_kernel(per_split, e_ref, g_ref, *refs):
    h_refs, a_refs = refs[:N_SPLIT], refs[N_SPLIT:]
    e = e_ref[...]
    in_part = [(e >= k * per_split) & (e < (k + 1) * per_split) for k in range(N_SPLIT)]
    h = h_refs[0][...]
    for k in range(1, N_SPLIT):
        h = jnp.where(in_part[k], h_refs[k][...], h)
    a = 0.5 * h * (1.0 + lax.erf(h * math.sqrt(0.5))) * g_ref[...]
    for k in range(N_SPLIT):
        a_refs[k][...] = jnp.where(in_part[k], a, 0.0).T


def _act(e, g, hs, per_split):
    npair, T = e.shape
    tt = min(1024, T)
    pair_major = pl.BlockSpec((npair, tt), lambda i: (0, i))
    token_major = pl.BlockSpec((tt, npair), lambda i: (i, 0))
    return pl.pallas_call(
        functools.partial(_act_kernel, per_split),
        out_shape=tuple(jax.ShapeDtypeStruct((T, npair), F32) for _ in range(N_SPLIT)),
        grid=(T // tt,), in_specs=[pair_major] * (2 + N_SPLIT),
        out_specs=tuple([token_major] * N_SPLIT),
        compiler_params=_params(("parallel",)), name="act",
    )(e, g, *hs)


def _yside_kernel(part, has_prev, npair, bnd_ref, idx_ref, a_ref, v_ref, *refs):
    y_ref, items_ref, part_ref = refs[-3], refs[-2], refs[-1]
    tt = y_ref.shape[0]
    groups = npair // PAIR_GROUP
    n_acc = 4
    nblk = _build_items(part, bnd_ref, items_ref, tt, groups)

    @pl.when(pl.program_id(0) == 0)
    def _():
        part_ref[...] = jnp.zeros(part_ref.shape, F32)

    def block(i, carry):
        for q in range(ITEM_UNROLL):
            it = items_ref[i * ITEM_UNROLL + q]
            base = it * PAIR_GROUP
            accs = [None] * n_acc
            for r in range(PAIR_GROUP):
                term = a_ref[base + r] * v_ref[idx_ref[base + r]].astype(F32)
                accs[r % n_acc] = term if accs[r % n_acc] is None else accs[r % n_acc] + term
            part_ref[it] = (accs[0] + accs[1]) + (accs[2] + accs[3])
        return carry

    lax.fori_loop(0, nblk, block, 0)

    def finish(t, carry):
        first, last = bnd_ref[SUBLANES * t + 2 * part], bnd_ref[SUBLANES * t + 2 * part + 1]
        y = refs[0][t] if has_prev else jnp.zeros(v_ref.shape[1:], F32)
        for g in range(groups):
            y = y + jnp.where((g >= first) & (g < last), part_ref[t * groups + g], 0.0)
        y_ref[t] = y
        return carry

    lax.fori_loop(0, tt, finish, 0, unroll=4)


def _yside(part, bnd, idx, a, table, prev, npair):
    T = idx.shape[0] // npair
    tt = TT_PEER
    tile = pl.BlockSpec((tt,) + table.shape[1:], lambda i: (i, 0, 0))
    bnd_spec, idx_spec = _peer_specs(npair, tt)
    in_specs = [bnd_spec, idx_spec, idx_spec, pl.BlockSpec(memory_space=pltpu.VMEM)]
    args = [bnd, idx, a, table]
    if prev is not None:
        in_specs.append(tile)
        args.append(prev)
    return pl.pallas_call(
        functools.partial(_yside_kernel, part, prev is not None, npair),
        out_shape=jax.ShapeDtypeStruct((T,) + table.shape[1:], F32), grid=(T // tt,),
        in_specs=in_specs, out_specs=tile,
        scratch_shapes=[_items_scratch(npair, tt),
                        pltpu.VMEM((tt * (npair // PAIR_GROUP),) + table.shape[1:], F32)],
        compiler_params=_params(("arbitrary",)), name="yside",
    )(*args)


def _ln2_kernel(alpha, x_ref, y_ref, g_ref, b_ref, o_ref):
    o_ref[...] = _layer_norm(alpha * x_ref[...] + y_ref[...], g_ref[...], b_ref[...])


def _ln2(x1, y, g, b, alpha):
    T, D = x1.shape
    tm = 512
    tile = pl.BlockSpec((tm, D), lambda i: (i, 0))
    vec = pl.BlockSpec((1, D), lambda i: (0, 0))
    return pl.pallas_call(
        functools.partial(_ln2_kernel, alpha), out_shape=jax.ShapeDtypeStruct((T, D), F32),
        grid=(T // tm,), in_specs=[tile, tile, vec, vec], out_specs=tile,
        compiler_params=_params(("parallel",)), name="ln2")(x1, y, g, b)


def _rope_tables(positions):
    T = positions.size
    half = ROT_DIM // 2
    inv_freq = ROPE_THETA ** (-jnp.arange(0, ROT_DIM, 2, dtype=F32) / ROT_DIM)
    ang = positions.reshape(T, 1).astype(F32) * inv_freq
    cos, sin = jnp.cos(ang), jnp.sin(ang)
    rest = HEAD_DIM - ROT_DIM
    zeros_h = jnp.zeros((T, half), F32)
    c = jnp.concatenate([cos, cos, jnp.ones((T, rest), F32)], axis=-1)
    s1 = jnp.concatenate([-sin, zeros_h, jnp.zeros((T, rest), F32)], axis=-1)
    s2 = jnp.concatenate([zeros_h, sin, jnp.zeros((T, rest), F32)], axis=-1)
    rep = LANES // HEAD_DIM
    return tuple(jnp.tile(t, (1, rep)) for t in (c, s1, s2))


def kernel(x, positions, w_in, b_in, conv_dw_w, conv_dw_b, conv_ln_g, conv_ln_b, w_conv_out, attn_sinks, w_mix_out, ln1_g, ln1_b, w_peer_q, sub_keys_1, sub_keys_2, expert_u, expert_v, ln2_g, ln2_b):
    B, S, D = x.shape
    T = B * S
    depth = w_in.shape[0]
    alpha = (2 * depth) ** 0.25
    dc = conv_dw_w.shape[-1]
    qw = N_HEADS * HEAD_DIM
    kvw = N_KV_HEADS * HEAD_DIM
    n_exp = expert_u.shape[1]
    per_split = n_exp // N_SPLIT
    packed = (D // LANES, LANES)
    c, s1, s2 = _rope_tables(positions)
    row = lambda v: v.reshape(1, -1)

    x2 = x.reshape(T, D)
    for l in range(depth):
        w = w_in[l].astype(BF16)
        b = b_in[l]
        o_q = 2 * dc
        o_k, o_v, o_ga = o_q + qw, o_q + qw + kvw, o_q + qw + 2 * kvw
        u = _glu(x2, w[:, :dc], w[:, dc:2 * dc], row(b[:dc]), row(b[dc:2 * dc]))
        qkv = _qkv(x2, w[:, o_q:o_ga], row(b[o_q:o_ga]), c, s1, s2, qw + kvw)
        gv = _gates(x2, w[:, o_ga:], row(b[o_ga:]))
        ya = _attention(qkv, attn_sinks[l], B, S)
        x1, x1b = _mix(u, conv_dw_w[l].reshape(-1, dc), row(conv_dw_b[l]), row(conv_ln_g[l]),
                       row(conv_ln_b[l]), w_conv_out[l].astype(BF16), gv, ya,
                       w_mix_out[l].astype(BF16), x2, row(ln1_g[l]), row(ln1_b[l]), alpha, B, S)
        outs = _topk(x1b, w_peer_q[l].astype(BF16), sub_keys_1[l].astype(BF16),
                     sub_keys_2[l].astype(BF16), per_split)
        e, g = outs[0], outs[1]
        npair = e.shape[0]
        bnd = outs[2].T.reshape(-1)
        idxs = [i.reshape(-1) for i in outs[3:]]
        x1p = x1b.reshape((T,) + packed)
        ut = expert_u[l].astype(BF16).reshape((N_SPLIT, per_split) + packed)
        vt = expert_v[l].astype(BF16).reshape((N_SPLIT, per_split) + packed)
        hs = [_hside(k, bnd, idxs[k], x1p, ut[k], npair) for k in range(N_SPLIT)]
        acts = _act(e, g, hs, per_split)
        y = None
        for k in range(N_SPLIT):
            y = _yside(k, bnd, idxs[k], acts[k].reshape(-1), vt[k], y, npair)
        x2 = _ln2(x1, y.reshape(T, D), row(ln2_g[l]), row(ln2_b[l]), alpha)
    return x2.reshape(B, S, D)
```

```python
import functools
import math

import jax
import jax.numpy as jnp
from jax import lax
from jax.experimental import pallas as pl
from jax.experimental.pallas import tpu as pltpu

F32 = jnp.float32
BF16 = jnp.bfloat16

N_HEADS = 32
N_KV_HEADS = 4
GROUP = N_HEADS // N_KV_HEADS
HEAD_DIM = 64
WINDOW = 128
ROPE_THETA = 500000.0
ROT_DIM = HEAD_DIM // 4
ATTN_SCALE = 1.0 / math.sqrt(HEAD_DIM)
PEER_HEADS = 8
N_KEYS = 128
TOPK = 16
LN_EPS = 1e-5

LANES = 128
SUBLANES = 8
VMEM_LIMIT = 56 * 1024 * 1024

TM_PROJ = 1024
TN_PROJ = 512
TS_MIX = 256
HALO = 32
TT_TOPK = 256
TT_PEER = 128
N_SPLIT = 2
PAIR_GROUP = 16
ITEM_UNROLL = 4


def _params(sem):
    return pltpu.CompilerParams(dimension_semantics=sem, vmem_limit_bytes=VMEM_LIMIT)


def _glu_kernel(x_ref, wa_ref, wb_ref, ba_ref, bb_ref, o_ref, xb_ref):
    @pl.when(pl.program_id(1) == 0)
    def _():
        xb_ref[...] = x_ref[...].astype(BF16)

    xb = xb_ref[...]
    za = jnp.dot(xb, wa_ref[...], preferred_element_type=F32) + ba_ref[...]
    zb = jnp.dot(xb, wb_ref[...], preferred_element_type=F32) + bb_ref[...]
    o_ref[...] = za * jax.nn.sigmoid(zb)


def _rope_block(z, c, s1, s2):
    half = ROT_DIM // 2
    return z * c + pltpu.roll(z, LANES - half, axis=1) * s1 + pltpu.roll(z, half, axis=1) * s2


def _qkv_kernel(n_rope_cols, x_ref, w_ref, b_ref, c_ref, s1_ref, s2_ref, o_ref, xb_ref):
    j = pl.program_id(1)

    @pl.when(j == 0)
    def _():
        xb_ref[...] = x_ref[...].astype(BF16)

    z = jnp.dot(xb_ref[...], w_ref[...], preferred_element_type=F32) + b_ref[...]
    c, s1, s2 = c_ref[...], s1_ref[...], s2_ref[...]
    tn = z.shape[1]
    for k in range(tn // LANES):
        sl = slice(LANES * k, LANES * (k + 1))
        rotary = j * tn + LANES * k < n_rope_cols
        o_ref[:, sl] = jnp.where(rotary, _rope_block(z[:, sl], c, s1, s2), z[:, sl]).astype(o_ref.dtype)


def _gates_kernel(x_ref, w_ref, b_ref, o_ref, xb_ref):
    @pl.when(pl.program_id(1) == 0)
    def _():
        xb_ref[...] = x_ref[...].astype(BF16)

    z = jnp.dot(xb_ref[...], w_ref[...], preferred_element_type=F32) + b_ref[...]
    o_ref[...] = jax.nn.sigmoid(z).astype(o_ref.dtype)


def _proj_specs(T, D, N, n_w, extra_specs=()):
    tm, tn = TM_PROJ, TN_PROJ
    assert T % tm == 0 and N % tn == 0, (T, N)
    grid = (T // tm, N // tn)
    in_specs = [pl.BlockSpec((tm, D), lambda i, j: (i, 0))]
    in_specs += [pl.BlockSpec((D, tn), lambda i, j: (0, j)) for _ in range(n_w)]
    in_specs += [pl.BlockSpec((1, tn), lambda i, j: (0, j)) for _ in range(n_w)]
    in_specs += list(extra_specs)
    out_spec = pl.BlockSpec((tm, tn), lambda i, j: (i, j))
    scratch = [pltpu.VMEM((tm, D), BF16)]
    return grid, in_specs, out_spec, scratch


def _glu(x2, wa, wb, ba, bb):
    T, D = x2.shape
    N = wa.shape[1]
    grid, in_specs, out_spec, scratch = _proj_specs(T, D, N, 2)
    return pl.pallas_call(
        _glu_kernel, out_shape=jax.ShapeDtypeStruct((T, N), F32), grid=grid,
        in_specs=in_specs, out_specs=out_spec, scratch_shapes=scratch,
        compiler_params=_params(("parallel", "arbitrary")), name="glu")(x2, wa, wb, ba, bb)


def _qkv(x2, w, b, c, s1, s2, n_rope_cols):
    T, D = x2.shape
    N = w.shape[1]
    tab = pl.BlockSpec((TM_PROJ, LANES), lambda i, j: (i, 0))
    grid, in_specs, out_spec, scratch = _proj_specs(T, D, N, 1, (tab, tab, tab))
    return pl.pallas_call(
        functools.partial(_qkv_kernel, n_rope_cols),
        out_shape=jax.ShapeDtypeStruct((T, N), BF16), grid=grid,
        in_specs=in_specs, out_specs=out_spec, scratch_shapes=scratch,
        compiler_params=_params(("parallel", "arbitrary")), name="qkv")(x2, w, b, c, s1, s2)


def _gates(x2, w, b):
    T, D = x2.shape
    N = w.shape[1]
    grid, in_specs, out_spec, scratch = _proj_specs(T, D, N, 1)
    return pl.pallas_call(
        _gates_kernel, out_shape=jax.ShapeDtypeStruct((T, N), BF16), grid=grid,
        in_specs=in_specs, out_specs=out_spec, scratch_shapes=scratch,
        compiler_params=_params(("parallel", "arbitrary")), name="gates")(x2, w, b)


def _attn_kernel(sink_ref, q_ref, kc_ref, kp_ref, vc_ref, vp_ref, o_ref):
    n = pl.program_id(1)
    blk = WINDOW
    lane = lax.broadcasted_iota(jnp.int32, (blk, LANES), 1)
    lo = lane < HEAD_DIM
    lane2 = lax.broadcasted_iota(jnp.int32, (2 * blk, LANES), 1)
    lo2 = lane2 < HEAD_DIM
    qi = lax.broadcasted_iota(jnp.int32, (blk, 2 * blk), 0)
    sj = lax.broadcasted_iota(jnp.int32, (blk, 2 * blk), 1)
    valid = (sj > qi) & (sj <= qi + blk) & (sj >= blk * (1 - n))
    neg = jnp.finfo(F32).min
    kk = jnp.concatenate([kp_ref[...], kc_ref[...]], axis=0).astype(F32)
    vv = jnp.concatenate([vp_ref[...], vc_ref[...]], axis=0).astype(F32)
    heads_per_tile = LANES // HEAD_DIM
    tiles_per_group = GROUP // heads_per_tile
    for g in range(N_KV_HEADS):
        tsl = slice(LANES * (g // heads_per_tile), LANES * (g // heads_per_tile + 1))

        def both_halves(t):
            r = pltpu.roll(t, HEAD_DIM, axis=1)
            return (jnp.where(lo2, t, r) if g % heads_per_tile == 0 else jnp.where(lo2, r, t)).astype(BF16)

        kd = both_halves(kk[:, tsl])
        vd = both_halves(vv[:, tsl])
        qs = []
        for c in range(tiles_per_group):
            col = LANES * (tiles_per_group * g + c)
            qt = q_ref[:, col:col + LANES]
            zero = jnp.zeros_like(qt)
            qs.append(jnp.where(lo, qt, zero))
            qs.append(jnp.where(lo, zero, qt))
        q8 = jnp.concatenate(qs, axis=0)
        s = lax.dot_general(q8, kd, (((1,), (1,)), ((), ())), preferred_element_type=F32) * ATTN_SCALE
        s3 = jnp.where(valid[None], s.reshape(GROUP, blk, 2 * blk), neg)
        sink = sink_ref[GROUP * g:GROUP * (g + 1)]
        m = jnp.maximum(jnp.max(s3, axis=-1, keepdims=True), sink)
        p = jnp.exp(s3 - m)
        denom = jnp.sum(p, axis=-1, keepdims=True) + jnp.exp(sink - m)
        p = p / denom
        o = jnp.dot(p.reshape(GROUP * blk, 2 * blk).astype(BF16), vd, preferred_element_type=F32)
        for c in range(tiles_per_group):
            oa = o[blk * (2 * c):blk * (2 * c + 1)]
            ob = o[blk * (2 * c + 1):blk * (2 * c + 2)]
            col = LANES * (tiles_per_group * g + c)
            o_ref[:, col:col + LANES] = jnp.where(lo, oa, ob).astype(o_ref.dtype)


def _attention(qkv, sinks, B, S):
    T = B * S
    blk = WINDOW
    nb = S // blk
    qw = N_HEADS * HEAD_DIM
    kvw = N_KV_HEADS * HEAD_DIM
    k_col = qw // kvw
    v_col = k_col + 1
    row = lambda b, n: b * nb + n
    prev = lambda b, n: jnp.maximum(b * nb + n - 1, 0)
    in_specs = [
        pl.BlockSpec((N_HEADS, 1, 1), lambda b, n: (0, 0, 0)),
        pl.BlockSpec((blk, qw), lambda b, n: (row(b, n), 0)),
        pl.BlockSpec((blk, kvw), lambda b, n: (row(b, n), k_col)),
        pl.BlockSpec((blk, kvw), lambda b, n: (prev(b, n), k_col)),
        pl.BlockSpec((blk, kvw), lambda b, n: (row(b, n), v_col)),
        pl.BlockSpec((blk, kvw), lambda b, n: (prev(b, n), v_col)),
    ]
    return pl.pallas_call(
        _attn_kernel, out_shape=jax.ShapeDtypeStruct((T, qw), BF16), grid=(B, nb),
        in_specs=in_specs, out_specs=pl.BlockSpec((blk, qw), lambda b, n: (row(b, n), 0)),
        compiler_params=_params(("parallel", "arbitrary")), name="attn",
    )(sinks.astype(F32).reshape(N_HEADS, 1, 1), qkv, qkv, qkv, qkv, qkv)


def _layer_norm(x, g, b):
    mu = jnp.mean(x, axis=-1, keepdims=True)
    xc = x - mu
    var = jnp.mean(xc * xc, axis=-1, keepdims=True)
    return xc * lax.rsqrt(var + LN_EPS) * g + b


def _mix_kernel(alpha, u_ref, uh_ref, cw_ref, cb_ref, cg_ref, cbeta_ref, wco_ref, sa_ref, sb_ref,
                ya_ref, wmo_ref, x_ref, g1_ref, b1_ref, x1_ref, x1b_ref, win_ref, conv_ref):
    s = pl.program_id(1)
    ts, dc = u_ref.shape
    width = cw_ref.shape[0]
    win_ref[0:HALO, :] = jnp.where(s == 0, 0.0, uh_ref[...])
    win_ref[HALO:HALO + ts, :] = u_ref[...]
    first = HALO - (width - 1)
    for c in range(dc // LANES):
        cs = slice(LANES * c, LANES * (c + 1))
        acc = jnp.broadcast_to(cb_ref[:, cs], (ts, LANES))
        for j in range(width):
            acc = acc + cw_ref[j:j + 1, cs] * win_ref[first + j:first + j + ts, cs]
        conv_ref[:, cs] = acc
    h = _layer_norm(conv_ref[...], cg_ref[...], cbeta_ref[...])
    h = h * jax.nn.sigmoid(h)
    yc = jnp.dot(h.astype(BF16), wco_ref[...], preferred_element_type=F32)
    m = sa_ref[...].astype(F32) * yc + sb_ref[...].astype(F32) * ya_ref[...].astype(F32)
    mix = jnp.dot(m.astype(BF16), wmo_ref[...], preferred_element_type=F32)
    x1 = _layer_norm(alpha * x_ref[...] + mix, g1_ref[...], b1_ref[...])
    x1_ref[...] = x1
    x1b_ref[...] = x1.astype(BF16)


def _mix(u, cw, cb, cg, cbeta, wco, gv, ya, wmo, x2, g1, b1, alpha, B, S):
    T, D = x2.shape
    dc = u.shape[1]
    ts = TS_MIX
    ns = S // ts
    row = lambda b, s: b * ns + s
    halo = lambda b, s: jnp.maximum((b * S + s * ts) // HALO - 1, 0)
    const = lambda shape: pl.BlockSpec(shape, lambda b, s: (0, 0))
    in_specs = [
        pl.BlockSpec((ts, dc), lambda b, s: (row(b, s), 0)),
        pl.BlockSpec((HALO, dc), lambda b, s: (halo(b, s), 0)),
        const(cw.shape), const((1, dc)), const((1, dc)), const((1, dc)),
        const(wco.shape),
        pl.BlockSpec((ts, D), lambda b, s: (row(b, s), 0)),
        pl.BlockSpec((ts, D), lambda b, s: (row(b, s), 1)),
        pl.BlockSpec((ts, D), lambda b, s: (row(b, s), 0)),
        const(wmo.shape),
        pl.BlockSpec((ts, D), lambda b, s: (row(b, s), 0)),
        const((1, D)), const((1, D)),
    ]
    out_spec = pl.BlockSpec((ts, D), lambda b, s: (row(b, s), 0))
    return pl.pallas_call(
        functools.partial(_mix_kernel, alpha),
        out_shape=(jax.ShapeDtypeStruct((T, D), F32), jax.ShapeDtypeStruct((T, D), BF16)),
        grid=(B, ns), in_specs=in_specs, out_specs=(out_spec, out_spec),
        scratch_shapes=[pltpu.VMEM((HALO + ts, dc), F32), pltpu.VMEM((ts, dc), F32)],
        compiler_params=_params(("parallel", "arbitrary")), name="mix",
    )(u, u, cw, cb, cg, cbeta, wco, gv, gv, ya, wmo, x2, g1, b1)


def _topk_axis0(s, k, payload=None, rank=None):
    iota = lax.broadcasted_iota(jnp.int32, s.shape, 0) if rank is None else rank
    big = jnp.iinfo(jnp.int32).max
    vals, outs = [], []
    for _ in range(k):
        m = jnp.max(s, axis=0, keepdims=True)
        idx = jnp.min(jnp.where(s == m, iota, big), axis=0, keepdims=True)
        sel = iota == idx
        vals.append(m)
        if payload is None:
            outs.append(idx)
        else:
            outs.append(jnp.max(jnp.where(sel, payload, -1), axis=0, keepdims=True))
        s = jnp.where(sel, -jnp.inf, s)
    return jnp.concatenate(vals, axis=0), jnp.concatenate(outs, axis=0)


def _candidate_grid(v1, i1, v2, i2):
    k, t = v1.shape
    sub = lax.broadcasted_iota(jnp.int32, (SUBLANES, t), 0)
    vals, ids, flats = [], [], []
    for a in range(k // 2):
        nb = k // (a + 1)
        for b0 in range(0, nb, SUBLANES):
            live = sub < (nb - b0)
            vals.append(jnp.where(live, v1[a:a + 1] + v2[b0:b0 + SUBLANES], -jnp.inf))
            ids.append(i1[a:a + 1] * N_KEYS + i2[b0:b0 + SUBLANES])
            flats.append(a * k + b0 + sub)
    for a0 in range(k // 2, k, SUBLANES):
        vals.append(v1[a0:a0 + SUBLANES] + v2[0:1])
        ids.append(i1[a0:a0 + SUBLANES] * N_KEYS + i2[0:1])
        flats.append((a0 + sub) * k)
    return jnp.concatenate(vals, axis=0), jnp.concatenate(ids, axis=0), jnp.concatenate(flats, axis=0)


def _partner_rows(x, j):
    n = x.shape[0]
    if j >= SUBLANES:
        return jnp.concatenate([x[(b ^ 1) * j:((b ^ 1) + 1) * j] for b in range(n // j)], axis=0)
    sub = lax.broadcasted_iota(jnp.int32, (SUBLANES, x.shape[1]), 0)
    low = (sub & j) == 0
    out = []
    for r in range(n // SUBLANES):
        slab = x[SUBLANES * r:SUBLANES * (r + 1)]
        out.append(jnp.where(low, pltpu.roll(slab, SUBLANES - j, axis=0), pltpu.roll(slab, j, axis=0)))
    return jnp.concatenate(out, axis=0)


def _sort_axis0(key, val):
    n = key.shape[0]
    row = lax.broadcasted_iota(jnp.int32, key.shape, 0)
    k = 2
    while k <= n:
        j = k // 2
        while j >= 1:
            pk, pv = _partner_rows(key, j), _partner_rows(val, j)
            sign = 1 - 2 * (((row & j) != 0).astype(jnp.int32) ^ ((row & k) != 0).astype(jnp.int32))
            take = (pk - key) * sign < 0
            key = jnp.where(take, pk, key)
            val = jnp.where(take, pv, val)
            j //= 2
        k *= 2
    return key, val


def _topk_kernel(per_split, x_ref, wq_ref, k1_ref, k2_ref, e_ref, g_ref, bnd_ref, *idx_refs):
    q = jnp.dot(x_ref[...], wq_ref[...], preferred_element_type=F32).astype(BF16)
    nt = (((1,), (1,)), ((), ()))
    half = k1_ref.shape[1]
    es, gs = [], []
    for h in range(PEER_HEADS):
        q1 = q[:, 2 * half * h:2 * half * h + half]
        q2 = q[:, 2 * half * h + half:2 * half * (h + 1)]
        s1 = lax.dot_general(k1_ref[...], q1, nt, preferred_element_type=F32)
        s2 = lax.dot_general(k2_ref[...], q2, nt, preferred_element_type=F32)
        v1, i1 = _topk_axis0(s1, TOPK)
        v2, i2 = _topk_axis0(s2, TOPK)
        cand, cidx, flat = _candidate_grid(v1, i1, v2, i2)
        best, eidx = _topk_axis0(cand, TOPK, payload=cidx, rank=flat)
        w = jnp.exp(best - best[0:1])
        es.append(eidx)
        gs.append(w / jnp.sum(w, axis=0, keepdims=True))
    e, g = _sort_axis0(jnp.concatenate(es, axis=0), jnp.concatenate(gs, axis=0))
    e_ref[...] = e
    g_ref[...] = g
    shift = PAIR_GROUP.bit_length() - 1
    bounds = []
    for part, idx_ref in enumerate(idx_refs):
        idx_ref[...] = (jnp.clip(e - part * per_split, 0, per_split - 1) * SUBLANES).T
        below = jnp.sum((e < part * per_split).astype(jnp.int32), axis=0, keepdims=True)
        upto = jnp.sum((e < (part + 1) * per_split).astype(jnp.int32), axis=0, keepdims=True)
        bounds += [below >> shift, (upto + (PAIR_GROUP - 1)) >> shift]
    pad = bnd_ref.shape[0] - len(bounds)
    bnd_ref[...] = jnp.concatenate(bounds + [jnp.zeros_like(bounds[0])] * pad, axis=0)


def _topk(x1b, wq, k1, k2, per_split):
    T, D = x1b.shape
    tt = TT_TOPK
    npair = PEER_HEADS * TOPK
    const = lambda shape: pl.BlockSpec(shape, lambda i: (0, 0))
    pair_major = pl.BlockSpec((npair, tt), lambda i: (0, i))
    token_major = pl.BlockSpec((tt, npair), lambda i: (i, 0))
    return pl.pallas_call(
        functools.partial(_topk_kernel, per_split),
        out_shape=(jax.ShapeDtypeStruct((npair, T), jnp.int32), jax.ShapeDtypeStruct((npair, T), F32),
                   jax.ShapeDtypeStruct((SUBLANES, T), jnp.int32))
        + tuple(jax.ShapeDtypeStruct((T, npair), jnp.int32) for _ in range(N_SPLIT)),
        grid=(T // tt,),
        in_specs=[pl.BlockSpec((tt, D), lambda i: (i, 0)), const(wq.shape), const(k1.shape), const(k2.shape)],
        out_specs=(pair_major, pair_major, pl.BlockSpec((SUBLANES, tt), lambda i: (0, i)))
        + tuple(token_major for _ in range(N_SPLIT)),
        compiler_params=_params(("parallel",)), name="topk",
    )(x1b, wq, k1, k2)


def _build_items(part, bnd_ref, items_ref, tt, groups):
    def per_token(t, n):
        first, last = bnd_ref[SUBLANES * t + 2 * part], bnd_ref[SUBLANES * t + 2 * part + 1]
        for g in range(groups):
            items_ref[n + g] = t * groups + first + g
        return n + (last - first)

    n = lax.fori_loop(0, tt, per_token, 0)
    final = items_ref[jnp.maximum(n - 1, 0)]
    for q in range(ITEM_UNROLL - 1):
        items_ref[n + q] = final
    return lax.shift_right_logical(n + (ITEM_UNROLL - 1), ITEM_UNROLL.bit_length() - 1)


def _pack_table(table, pieces):
    n, d = table.shape
    bits = lax.bitcast_convert_type(table.astype(BF16), jnp.uint16).astype(jnp.uint32)
    words = bits[:, :d // 2] | (bits[:, d // 2:] << 16)
    return words.reshape(pieces, (n // pieces) * SUBLANES, LANES)


def _table_row(tab_ref, row8):
    words = tab_ref[pl.ds(pl.multiple_of(row8, SUBLANES), SUBLANES), :]
    lo = pltpu.bitcast(words << 16, F32)
    hi = pltpu.bitcast(words & jnp.uint32(0xFFFF0000), F32)
    return lo, hi


def _rows_to_sublanes(parts):
    sub = lax.broadcasted_iota(jnp.int32, (SUBLANES, LANES), 0)
    bits = SUBLANES.bit_length() - 1
    parts = [parts[int(format(i, f"0{bits}b")[::-1], 2)] for i in range(SUBLANES)]
    step = SUBLANES // 2
    while len(parts) > 1:
        low = (sub & step) == 0
        nxt = []
        for a, b in zip(parts[0::2], parts[1::2]):
            if 2 * step == SUBLANES:
                nxt.append(jnp.where(low, a, b) + pltpu.roll(jnp.where(low, b, a), step, axis=0))
            else:
                t = jnp.where(low, a, pltpu.roll(b, step, axis=0))
                u = jnp.where(low, pltpu.roll(a, SUBLANES - step, axis=0), b)
                nxt.append(t + u)
        parts = nxt
        step //= 2
    return parts[0]


def _hside_kernel(part, bnd_ref, idx_ref, x_ref, u_ref, h_ref, items_ref, sums_ref):
    npair, tt = h_ref.shape
    groups = npair // PAIR_GROUP
    gshift = groups.bit_length() - 1
    nblk = _build_items(part, bnd_ref, items_ref, tt, groups)

    @pl.when(pl.program_id(0) == 0)
    def _():
        sums_ref[...] = jnp.zeros(sums_ref.shape, F32)

    def block(i, carry):
        for q in range(ITEM_UNROLL):
            it = items_ref[i * ITEM_UNROLL + q]
            t = lax.shift_right_logical(it, gshift)
            base = it * PAIR_GROUP
            xf = x_ref[t].astype(F32)
            x_lo, x_hi = xf[0:SUBLANES], xf[SUBLANES:2 * SUBLANES]
            for sub in range(PAIR_GROUP // SUBLANES):
                parts = []
                for r in range(SUBLANES):
                    u_lo, u_hi = _table_row(u_ref, idx_ref[base + sub * SUBLANES + r])
                    parts.append(x_lo * u_lo + x_hi * u_hi)
                row0 = pl.multiple_of(base + sub * SUBLANES, SUBLANES)
                sums_ref[pl.ds(row0, SUBLANES), :] = _rows_to_sublanes(parts)
        return carry

    lax.fori_loop(0, nblk, block, 0)

    lane = lax.broadcasted_iota(jnp.int32, (npair, tt), 1)
    h_ref[...] = jnp.zeros(h_ref.shape, F32)

    def finish(t, carry):
        rows = sums_ref[pl.ds(pl.multiple_of(t * npair, npair), npair), :]
        h_ref[...] = jnp.where(lane == t, jnp.sum(rows, axis=1, keepdims=True), h_ref[...])
        return carry

    lax.fori_loop(0, tt, finish, 0, unroll=8)


def _peer_specs(npair, tt):
    bnd = pl.BlockSpec((tt * SUBLANES,), lambda i: (i,), memory_space=pltpu.SMEM)
    idx = pl.BlockSpec((tt * npair,), lambda i: (i,), memory_space=pltpu.SMEM)
    return bnd, idx


def _items_scratch(npair, tt):
    return pltpu.SMEM((tt * (npair // PAIR_GROUP) + PAIR_GROUP + ITEM_UNROLL,), jnp.int32)


def _hside(part, bnd, idx, x1p, table, npair):
    T = x1p.shape[0]
    tt = TT_PEER
    return pl.pallas_call(
        functools.partial(_hside_kernel, part),
        out_shape=jax.ShapeDtypeStruct((npair, T), F32), grid=(T // tt,),
        in_specs=[*_peer_specs(npair, tt),
                  pl.BlockSpec((tt,) + x1p.shape[1:], lambda i: (i, 0, 0)),
                  pl.BlockSpec(memory_space=pltpu.VMEM)],
        out_specs=pl.BlockSpec((npair, tt), lambda i: (0, i)),
        scratch_shapes=[_items_scratch(npair, tt), pltpu.VMEM((tt * npair, LANES), F32)],
        compiler_params=_params(("arbitrary",)), name="hside",
    )(bnd, idx, x1p, table)


def _act_kernel(per_split, e_ref, g_ref, *refs):
    h_refs, a_refs = refs[:N_SPLIT], refs[N_SPLIT:]
    e = e_ref[...]
    in_part = [(e >= k * per_split) & (e < (k + 1) * per_split) for k in range(N_SPLIT)]
    h = h_refs[0][...]
    for k in range(1, N_SPLIT):
        h = jnp.where(in_part[k], h_refs[k][...], h)
    a = 0.5 * h * (1.0 + lax.erf(h * math.sqrt(0.5))) * g_ref[...]
    for k in range(N_SPLIT):
        a_refs[k][...] = jnp.where(in_part[k], a, 0.0).T


def _act(e, g, hs, per_split):
    npair, T = e.shape
    tt = min(1024, T)
    pair_major = pl.BlockSpec((npair, tt), lambda i: (0, i))
    token_major = pl.BlockSpec((tt, npair), lambda i: (i, 0))
    return pl.pallas_call(
        functools.partial(_act_kernel, per_split),
        out_shape=tuple(jax.ShapeDtypeStruct((T, npair), F32) for _ in range(N_SPLIT)),
        grid=(T // tt,), in_specs=[pair_major] * (2 + N_SPLIT),
        out_specs=tuple([token_major] * N_SPLIT),
        compiler_params=_params(("parallel",)), name="act",
    )(e, g, *hs)


def _yside_kernel(part, has_prev, npair, bnd_ref, idx_ref, a_ref, v_ref, *refs):
    y_ref, items_ref, part_ref = refs[-3], refs[-2], refs[-1]
    tt = y_ref.shape[0]
    groups = npair // PAIR_GROUP
    n_acc = 4
    nblk = _build_items(part, bnd_ref, items_ref, tt, groups)

    @pl.when(pl.program_id(0) == 0)
    def _():
        part_ref[...] = jnp.zeros(part_ref.shape, F32)

    def block(i, carry):
        for q in range(ITEM_UNROLL):
            it = items_ref[i * ITEM_UNROLL + q]
            base = it * PAIR_GROUP
            los, his = [None] * n_acc, [None] * n_acc
            for r in range(PAIR_GROUP):
                w = a_ref[base + r]
                v_lo, v_hi = _table_row(v_ref, idx_ref[base + r])
                k = r % n_acc
                los[k] = w * v_lo if los[k] is None else los[k] + w * v_lo
                his[k] = w * v_hi if his[k] is None else his[k] + w * v_hi
            lo = (los[0] + los[1]) + (los[2] + los[3])
            hi = (his[0] + his[1]) + (his[2] + his[3])
            part_ref[it] = jnp.concatenate([lo, hi], axis=0)
        return carry

    lax.fori_loop(0, nblk, block, 0)

    group_id = lax.broadcasted_iota(jnp.int32, (groups,) + y_ref.shape[1:], 0)

    def finish(t, carry):
        first, last = bnd_ref[SUBLANES * t + 2 * part], bnd_ref[SUBLANES * t + 2 * part + 1]
        sums = part_ref[pl.ds(pl.multiple_of(t * groups, groups), groups)]
        live = (group_id - first).astype(jnp.uint32) < (last - first).astype(jnp.uint32)
        y = jnp.sum(jnp.where(live, sums, 0.0), axis=0)
        if has_prev:
            y = y + refs[0][t]
        y_ref[t] = y
        return carry

    lax.fori_loop(0, tt, finish, 0, unroll=4)


def _yside(part, bnd, idx, a, table, prev, npair, packed):
    T = idx.shape[0] // npair
    tt = TT_PEER
    tile = pl.BlockSpec((tt,) + packed, lambda i: (i, 0, 0))
    bnd_spec, idx_spec = _peer_specs(npair, tt)
    in_specs = [bnd_spec, idx_spec, idx_spec, pl.BlockSpec(memory_space=pltpu.VMEM)]
    args = [bnd, idx, a, table]
    if prev is not None:
        in_specs.append(tile)
        args.append(prev)
    return pl.pallas_call(
        functools.partial(_yside_kernel, part, prev is not None, npair),
        out_shape=jax.ShapeDtypeStruct((T,) + packed, F32), grid=(T // tt,),
        in_specs=in_specs, out_specs=tile,
        scratch_shapes=[_items_scratch(npair, tt),
                        pltpu.VMEM((tt * (npair // PAIR_GROUP),) + packed, F32)],
        compiler_params=_params(("arbitrary",)), name="yside",
    )(*args)


def _ln2_kernel(alpha, x_ref, y_ref, g_ref, b_ref, o_ref):
    o_ref[...] = _layer_norm(alpha * x_ref[...] + y_ref[...], g_ref[...], b_ref[...])


def _ln2(x1, y, g, b, alpha):
    T, D = x1.shape
    tm = 512
    tile = pl.BlockSpec((tm, D), lambda i: (i, 0))
    vec = pl.BlockSpec((1, D), lambda i: (0, 0))
    return pl.pallas_call(
        functools.partial(_ln2_kernel, alpha), out_shape=jax.ShapeDtypeStruct((T, D), F32),
        grid=(T // tm,), in_specs=[tile, tile, vec, vec], out_specs=tile,
        compiler_params=_params(("parallel",)), name="ln2")(x1, y, g, b)


def _rope_tables(positions):
    T = positions.size
    half = ROT_DIM // 2
    inv_freq = ROPE_THETA ** (-jnp.arange(0, ROT_DIM, 2, dtype=F32) / ROT_DIM)
    ang = positions.reshape(T, 1).astype(F32) * inv_freq
    cos, sin = jnp.cos(ang), jnp.sin(ang)
    rest = HEAD_DIM - ROT_DIM
    zeros_h = jnp.zeros((T, half), F32)
    c = jnp.concatenate([cos, cos, jnp.ones((T, rest), F32)], axis=-1)
    s1 = jnp.concatenate([-sin, zeros_h, jnp.zeros((T, rest), F32)], axis=-1)
    s2 = jnp.concatenate([zeros_h, sin, jnp.zeros((T, rest), F32)], axis=-1)
    rep = LANES // HEAD_DIM
    return tuple(jnp.tile(t, (1, rep)) for t in (c, s1, s2))


def kernel(x, positions, w_in, b_in, conv_dw_w, conv_dw_b, conv_ln_g, conv_ln_b, w_conv_out, attn_sinks, w_mix_out, ln1_g, ln1_b, w_peer_q, sub_keys_1, sub_keys_2, expert_u, expert_v, ln2_g, ln2_b):
    B, S, D = x.shape
    T = B * S
    depth = w_in.shape[0]
    alpha = (2 * depth) ** 0.25
    dc = conv_dw_w.shape[-1]
    qw = N_HEADS * HEAD_DIM
    kvw = N_KV_HEADS * HEAD_DIM
    n_exp = expert_u.shape[1]
    per_split = n_exp // N_SPLIT
    packed = (D // LANES, LANES)
    c, s1, s2 = _rope_tables(positions)
    row = lambda v: v.reshape(1, -1)

    x2 = x.reshape(T, D)
    for l in range(depth):
        w = w_in[l].astype(BF16)
        b = b_in[l]
        o_q = 2 * dc
        o_k, o_v, o_ga = o_q + qw, o_q + qw + kvw, o_q + qw + 2 * kvw
        u = _glu(x2, w[:, :dc], w[:, dc:2 * dc], row(b[:dc]), row(b[dc:2 * dc]))
        qkv = _qkv(x2, w[:, o_q:o_ga], row(b[o_q:o_ga]), c, s1, s2, qw + kvw)
        gv = _gates(x2, w[:, o_ga:], row(b[o_ga:]))
        ya = _attention(qkv, attn_sinks[l], B, S)
        x1, x1b = _mix(u, conv_dw_w[l].reshape(-1, dc), row(conv_dw_b[l]), row(conv_ln_g[l]),
                       row(conv_ln_b[l]), w_conv_out[l].astype(BF16), gv, ya,
                       w_mix_out[l].astype(BF16), x2, row(ln1_g[l]), row(ln1_b[l]), alpha, B, S)
        outs = _topk(x1b, w_peer_q[l].astype(BF16), sub_keys_1[l].astype(BF16),
                     sub_keys_2[l].astype(BF16), per_split)
        e, g = outs[0], outs[1]
        npair = e.shape[0]
        bnd = outs[2].T.reshape(-1)
        idxs = [i.reshape(-1) for i in outs[3:]]
        x1p = x1b.reshape((T,) + packed)
        ut = _pack_table(expert_u[l], N_SPLIT)
        vt = _pack_table(expert_v[l], N_SPLIT)
        hs = [_hside(k, bnd, idxs[k], x1p, ut[k], npair) for k in range(N_SPLIT)]
        acts = _act(e, g, hs, per_split)
        y = None
        for k in range(N_SPLIT):
            y = _yside(k, bnd, idxs[k], acts[k].reshape(-1), vt[k], y, npair, packed)
        x2 = _ln2(x1, y.reshape(T, D), row(ln2_g[l]), row(ln2_b[l]), alpha)
    return x2.reshape(B, S, D)
```

```python
import functools
import math

import jax
import jax.numpy as jnp
from jax import lax
from jax.experimental import pallas as pl
from jax.experimental.pallas import tpu as pltpu

F32 = jnp.float32
BF16 = jnp.bfloat16

N_HEADS = 32
N_KV_HEADS = 4
GROUP = N_HEADS // N_KV_HEADS
HEAD_DIM = 64
WINDOW = 128
ROPE_THETA = 500000.0
ROT_DIM = HEAD_DIM // 4
ATTN_SCALE = 1.0 / math.sqrt(HEAD_DIM)
PEER_HEADS = 8
N_KEYS = 128
TOPK = 16
LN_EPS = 1e-5

LANES = 128
SUBLANES = 8
VMEM_LIMIT = 56 * 1024 * 1024

TM_PROJ = 1024
TN_PROJ = 512
TS_MIX = 256
HALO = 32
TT_TOPK = 256
TT_PEER = 128
N_SPLIT = 2
PAIR_GROUP = 16
ITEM_UNROLL = 4


def _params(sem):
    return pltpu.CompilerParams(dimension_semantics=sem, vmem_limit_bytes=VMEM_LIMIT)


def _glu_kernel(x_ref, wa_ref, wb_ref, ba_ref, bb_ref, o_ref, xb_ref):
    @pl.when(pl.program_id(1) == 0)
    def _():
        xb_ref[...] = x_ref[...].astype(BF16)

    xb = xb_ref[...]
    za = jnp.dot(xb, wa_ref[...], preferred_element_type=F32) + ba_ref[...]
    zb = jnp.dot(xb, wb_ref[...], preferred_element_type=F32) + bb_ref[...]
    o_ref[...] = za * jax.nn.sigmoid(zb)


def _rope_block(z, c, s1, s2):
    half = ROT_DIM // 2
    return z * c + pltpu.roll(z, LANES - half, axis=1) * s1 + pltpu.roll(z, half, axis=1) * s2


def _qkv_kernel(n_rope_cols, x_ref, w_ref, b_ref, c_ref, s1_ref, s2_ref, o_ref, xb_ref):
    j = pl.program_id(1)

    @pl.when(j == 0)
    def _():
        xb_ref[...] = x_ref[...].astype(BF16)

    z = jnp.dot(xb_ref[...], w_ref[...], preferred_element_type=F32) + b_ref[...]
    c, s1, s2 = c_ref[...], s1_ref[...], s2_ref[...]
    tn = z.shape[1]
    for k in range(tn // LANES):
        sl = slice(LANES * k, LANES * (k + 1))
        rotary = j * tn + LANES * k < n_rope_cols
        o_ref[:, sl] = jnp.where(rotary, _rope_block(z[:, sl], c, s1, s2), z[:, sl]).astype(o_ref.dtype)


def _gates_kernel(x_ref, w_ref, b_ref, o_ref, xb_ref):
    @pl.when(pl.program_id(1) == 0)
    def _():
        xb_ref[...] = x_ref[...].astype(BF16)

    z = jnp.dot(xb_ref[...], w_ref[...], preferred_element_type=F32) + b_ref[...]
    o_ref[...] = jax.nn.sigmoid(z).astype(o_ref.dtype)


def _proj_specs(T, D, N, n_w, extra_specs=()):
    tm, tn = TM_PROJ, TN_PROJ
    assert T % tm == 0 and N % tn == 0, (T, N)
    grid = (T // tm, N // tn)
    in_specs = [pl.BlockSpec((tm, D), lambda i, j: (i, 0))]
    in_specs += [pl.BlockSpec((D, tn), lambda i, j: (0, j)) for _ in range(n_w)]
    in_specs += [pl.BlockSpec((1, tn), lambda i, j: (0, j)) for _ in range(n_w)]
    in_specs += list(extra_specs)
    out_spec = pl.BlockSpec((tm, tn), lambda i, j: (i, j))
    scratch = [pltpu.VMEM((tm, D), BF16)]
    return grid, in_specs, out_spec, scratch


def _glu(x2, wa, wb, ba, bb):
    T, D = x2.shape
    N = wa.shape[1]
    grid, in_specs, out_spec, scratch = _proj_specs(T, D, N, 2)
    return pl.pallas_call(
        _glu_kernel, out_shape=jax.ShapeDtypeStruct((T, N), F32), grid=grid,
        in_specs=in_specs, out_specs=out_spec, scratch_shapes=scratch,
        compiler_params=_params(("parallel", "arbitrary")), name="glu")(x2, wa, wb, ba, bb)


def _qkv(x2, w, b, c, s1, s2, n_rope_cols):
    T, D = x2.shape
    N = w.shape[1]
    tab = pl.BlockSpec((TM_PROJ, LANES), lambda i, j: (i, 0))
    grid, in_specs, out_spec, scratch = _proj_specs(T, D, N, 1, (tab, tab, tab))
    return pl.pallas_call(
        functools.partial(_qkv_kernel, n_rope_cols),
        out_shape=jax.ShapeDtypeStruct((T, N), BF16), grid=grid,
        in_specs=in_specs, out_specs=out_spec, scratch_shapes=scratch,
        compiler_params=_params(("parallel", "arbitrary")), name="qkv")(x2, w, b, c, s1, s2)


def _gates(x2, w, b):
    T, D = x2.shape
    N = w.shape[1]
    grid, in_specs, out_spec, scratch = _proj_specs(T, D, N, 1)
    return pl.pallas_call(
        _gates_kernel, out_shape=jax.ShapeDtypeStruct((T, N), BF16), grid=grid,
        in_specs=in_specs, out_specs=out_spec, scratch_shapes=scratch,
        compiler_params=_params(("parallel", "arbitrary")), name="gates")(x2, w, b)


def _attn_kernel(sink_ref, q_ref, kc_ref, kp_ref, vc_ref, vp_ref, o_ref):
    n = pl.program_id(1)
    blk = WINDOW
    lane = lax.broadcasted_iota(jnp.int32, (blk, LANES), 1)
    lo = lane < HEAD_DIM
    lane2 = lax.broadcasted_iota(jnp.int32, (2 * blk, LANES), 1)
    lo2 = lane2 < HEAD_DIM
    qi = lax.broadcasted_iota(jnp.int32, (blk, 2 * blk), 0)
    sj = lax.broadcasted_iota(jnp.int32, (blk, 2 * blk), 1)
    valid = (sj > qi) & (sj <= qi + blk) & (sj >= blk * (1 - n))
    neg = jnp.finfo(F32).min
    kk = jnp.concatenate([kp_ref[...], kc_ref[...]], axis=0).astype(F32)
    vv = jnp.concatenate([vp_ref[...], vc_ref[...]], axis=0).astype(F32)
    heads_per_tile = LANES // HEAD_DIM
    tiles_per_group = GROUP // heads_per_tile
    for g in range(N_KV_HEADS):
        tsl = slice(LANES * (g // heads_per_tile), LANES * (g // heads_per_tile + 1))

        def both_halves(t):
            r = pltpu.roll(t, HEAD_DIM, axis=1)
            return (jnp.where(lo2, t, r) if g % heads_per_tile == 0 else jnp.where(lo2, r, t)).astype(BF16)

        kd = both_halves(kk[:, tsl])
        vd = both_halves(vv[:, tsl])
        qs = []
        for c in range(tiles_per_group):
            col = LANES * (tiles_per_group * g + c)
            qt = q_ref[:, col:col + LANES]
            zero = jnp.zeros_like(qt)
            qs.append(jnp.where(lo, qt, zero))
            qs.append(jnp.where(lo, zero, qt))
        q8 = jnp.concatenate(qs, axis=0)
        s = lax.dot_general(q8, kd, (((1,), (1,)), ((), ())), preferred_element_type=F32) * ATTN_SCALE
        s3 = jnp.where(valid[None], s.reshape(GROUP, blk, 2 * blk), neg)
        sink = sink_ref[GROUP * g:GROUP * (g + 1)]
        m = jnp.maximum(jnp.max(s3, axis=-1, keepdims=True), sink)
        p = jnp.exp(s3 - m)
        denom = jnp.sum(p, axis=-1, keepdims=True) + jnp.exp(sink - m)
        p = p * (1.0 / denom)
        o = jnp.dot(p.reshape(GROUP * blk, 2 * blk).astype(BF16), vd, preferred_element_type=F32)
        for c in range(tiles_per_group):
            oa = o[blk * (2 * c):blk * (2 * c + 1)]
            ob = o[blk * (2 * c + 1):blk * (2 * c + 2)]
            col = LANES * (tiles_per_group * g + c)
            o_ref[:, col:col + LANES] = jnp.where(lo, oa, ob).astype(o_ref.dtype)


def _attention(qkv, sinks, B, S):
    T = B * S
    blk = WINDOW
    nb = S // blk
    qw = N_HEADS * HEAD_DIM
    kvw = N_KV_HEADS * HEAD_DIM
    k_col = qw // kvw
    v_col = k_col + 1
    row = lambda b, n: b * nb + n
    prev = lambda b, n: jnp.maximum(b * nb + n - 1, 0)
    in_specs = [
        pl.BlockSpec((N_HEADS, 1, 1), lambda b, n: (0, 0, 0)),
        pl.BlockSpec((blk, qw), lambda b, n: (row(b, n), 0)),
        pl.BlockSpec((blk, kvw), lambda b, n: (row(b, n), k_col)),
        pl.BlockSpec((blk, kvw), lambda b, n: (prev(b, n), k_col)),
        pl.BlockSpec((blk, kvw), lambda b, n: (row(b, n), v_col)),
        pl.BlockSpec((blk, kvw), lambda b, n: (prev(b, n), v_col)),
    ]
    return pl.pallas_call(
        _attn_kernel, out_shape=jax.ShapeDtypeStruct((T, qw), BF16), grid=(B, nb),
        in_specs=in_specs, out_specs=pl.BlockSpec((blk, qw), lambda b, n: (row(b, n), 0)),
        compiler_params=_params(("parallel", "arbitrary")), name="attn",
    )(sinks.astype(F32).reshape(N_HEADS, 1, 1), qkv, qkv, qkv, qkv, qkv)


def _layer_norm(x, g, b):
    mu = jnp.mean(x, axis=-1, keepdims=True)
    xc = x - mu
    var = jnp.mean(xc * xc, axis=-1, keepdims=True)
    return xc * lax.rsqrt(var + LN_EPS) * g + b


def _mix_kernel(alpha, u_ref, uh_ref, cw_ref, cb_ref, cg_ref, cbeta_ref, wco_ref, sa_ref, sb_ref,
                ya_ref, wmo_ref, x_ref, g1_ref, b1_ref, x1_ref, x1b_ref, win_ref, conv_ref, shift_ref):
    s = pl.program_id(1)
    ts, dc = u_ref.shape
    width = cw_ref.shape[0]
    win_ref[0:HALO, :] = jnp.where(s == 0, 0.0, uh_ref[...])
    win_ref[HALO:HALO + ts, :] = u_ref[...]
    first = HALO - (width - 1)
    span = ts + HALO - SUBLANES
    for c in range(dc // LANES):
        cs = slice(LANES * c, LANES * (c + 1))
        for r in range(1, SUBLANES):
            shift_ref[r, 0:span, :] = win_ref[r:r + span, cs]
        acc = jnp.broadcast_to(cb_ref[:, cs], (ts, LANES))
        for j in range(width):
            r, a = (first + j) % SUBLANES, (first + j) // SUBLANES * SUBLANES
            rows = win_ref[a:a + ts, cs] if r == 0 else shift_ref[r, a:a + ts, :]
            acc = acc + cw_ref[j:j + 1, cs] * rows
        conv_ref[:, cs] = acc
    h = _layer_norm(conv_ref[...], cg_ref[...], cbeta_ref[...])
    h = h * jax.nn.sigmoid(h)
    yc = jnp.dot(h.astype(BF16), wco_ref[...], preferred_element_type=F32)
    m = sa_ref[...].astype(F32) * yc + sb_ref[...].astype(F32) * ya_ref[...].astype(F32)
    mix = jnp.dot(m.astype(BF16), wmo_ref[...], preferred_element_type=F32)
    x1 = _layer_norm(alpha * x_ref[...] + mix, g1_ref[...], b1_ref[...])
    x1_ref[...] = x1
    x1b_ref[...] = x1.astype(BF16)


def _mix(u, cw, cb, cg, cbeta, wco, gv, ya, wmo, x2, g1, b1, alpha, B, S):
    T, D = x2.shape
    dc = u.shape[1]
    ts = TS_MIX
    ns = S // ts
    row = lambda b, s: b * ns + s
    halo = lambda b, s: jnp.maximum((b * S + s * ts) // HALO - 1, 0)
    const = lambda shape: pl.BlockSpec(shape, lambda b, s: (0, 0))
    in_specs = [
        pl.BlockSpec((ts, dc), lambda b, s: (row(b, s), 0)),
        pl.BlockSpec((HALO, dc), lambda b, s: (halo(b, s), 0)),
        const(cw.shape), const((1, dc)), const((1, dc)), const((1, dc)),
        const(wco.shape),
        pl.BlockSpec((ts, D), lambda b, s: (row(b, s), 0)),
        pl.BlockSpec((ts, D), lambda b, s: (row(b, s), 1)),
        pl.BlockSpec((ts, D), lambda b, s: (row(b, s), 0)),
        const(wmo.shape),
        pl.BlockSpec((ts, D), lambda b, s: (row(b, s), 0)),
        const((1, D)), const((1, D)),
    ]
    out_spec = pl.BlockSpec((ts, D), lambda b, s: (row(b, s), 0))
    return pl.pallas_call(
        functools.partial(_mix_kernel, alpha),
        out_shape=(jax.ShapeDtypeStruct((T, D), F32), jax.ShapeDtypeStruct((T, D), BF16)),
        grid=(B, ns), in_specs=in_specs, out_specs=(out_spec, out_spec),
        scratch_shapes=[pltpu.VMEM((HALO + ts, dc), F32), pltpu.VMEM((ts, dc), F32),
                        pltpu.VMEM((SUBLANES, HALO + ts, LANES), F32)],
        compiler_params=_params(("parallel", "arbitrary")), name="mix",
    )(u, u, cw, cb, cg, cbeta, wco, gv, gv, ya, wmo, x2, g1, b1)


def _topk_axis0(s, k, payload=None, rank=None):
    iota = lax.broadcasted_iota(jnp.int32, s.shape, 0) if rank is None else rank
    big = jnp.iinfo(jnp.int32).max
    vals, outs = [], []
    for _ in range(k):
        m = jnp.max(s, axis=0, keepdims=True)
        idx = jnp.min(jnp.where(s == m, iota, big), axis=0, keepdims=True)
        sel = iota == idx
        vals.append(m)
        if payload is None:
            outs.append(idx)
        else:
            outs.append(jnp.max(jnp.where(sel, payload, -1), axis=0, keepdims=True))
        s = jnp.where(sel, -jnp.inf, s)
    return jnp.concatenate(vals, axis=0), jnp.concatenate(outs, axis=0)


def _candidate_grid(v1, i1, v2, i2):
    k, t = v1.shape
    sub = lax.broadcasted_iota(jnp.int32, (SUBLANES, t), 0)
    vals, ids, flats = [], [], []
    for a in range(k // 2):
        nb = k // (a + 1)
        for b0 in range(0, nb, SUBLANES):
            live = sub < (nb - b0)
            vals.append(jnp.where(live, v1[a:a + 1] + v2[b0:b0 + SUBLANES], -jnp.inf))
            ids.append(i1[a:a + 1] * N_KEYS + i2[b0:b0 + SUBLANES])
            flats.append(a * k + b0 + sub)
    for a0 in range(k // 2, k, SUBLANES):
        vals.append(v1[a0:a0 + SUBLANES] + v2[0:1])
        ids.append(i1[a0:a0 + SUBLANES] * N_KEYS + i2[0:1])
        flats.append((a0 + sub) * k)
    return jnp.concatenate(vals, axis=0), jnp.concatenate(ids, axis=0), jnp.concatenate(flats, axis=0)


def _partner_rows(x, j):
    n = x.shape[0]
    if j >= SUBLANES:
        return jnp.concatenate([x[(b ^ 1) * j:((b ^ 1) + 1) * j] for b in range(n // j)], axis=0)
    sub = lax.broadcasted_iota(jnp.int32, (SUBLANES, x.shape[1]), 0)
    low = (sub & j) == 0
    out = []
    for r in range(n // SUBLANES):
        slab = x[SUBLANES * r:SUBLANES * (r + 1)]
        out.append(jnp.where(low, pltpu.roll(slab, SUBLANES - j, axis=0), pltpu.roll(slab, j, axis=0)))
    return jnp.concatenate(out, axis=0)


def _sort_axis0(key, val):
    n = key.shape[0]
    row = lax.broadcasted_iota(jnp.int32, key.shape, 0)
    k = 2
    while k <= n:
        j = k // 2
        while j >= 1:
            pk, pv = _partner_rows(key, j), _partner_rows(val, j)
            sign = 1 - 2 * (((row & j) != 0).astype(jnp.int32) ^ ((row & k) != 0).astype(jnp.int32))
            take = (pk - key) * sign < 0
            key = jnp.where(take, pk, key)
            val = jnp.where(take, pv, val)
            j //= 2
        k *= 2
    return key, val


def _topk_kernel(per_split, x_ref, wq_ref, k1_ref, k2_ref, e_ref, g_ref, bnd_ref, *idx_refs):
    q = jnp.dot(x_ref[...], wq_ref[...], preferred_element_type=F32).astype(BF16)
    nt = (((1,), (1,)), ((), ()))
    half = k1_ref.shape[1]
    es, gs = [], []
    for h in range(PEER_HEADS):
        q1 = q[:, 2 * half * h:2 * half * h + half]
        q2 = q[:, 2 * half * h + half:2 * half * (h + 1)]
        s1 = lax.dot_general(k1_ref[...], q1, nt, preferred_element_type=F32)
        s2 = lax.dot_general(k2_ref[...], q2, nt, preferred_element_type=F32)
        v1, i1 = _topk_axis0(s1, TOPK)
        v2, i2 = _topk_axis0(s2, TOPK)
        cand, cidx, flat = _candidate_grid(v1, i1, v2, i2)
        best, eidx = _topk_axis0(cand, TOPK, payload=cidx, rank=flat)
        w = jnp.exp(best - best[0:1])
        es.append(eidx)
        gs.append(w / jnp.sum(w, axis=0, keepdims=True))
    e, g = _sort_axis0(jnp.concatenate(es, axis=0), jnp.concatenate(gs, axis=0))
    e_ref[...] = e
    g_ref[...] = g
    shift = PAIR_GROUP.bit_length() - 1
    bounds = []
    for part, idx_ref in enumerate(idx_refs):
        idx_ref[...] = (jnp.clip(e - part * per_split, 0, per_split - 1) * SUBLANES).T
        below = jnp.sum((e < part * per_split).astype(jnp.int32), axis=0, keepdims=True)
        upto = jnp.sum((e < (part + 1) * per_split).astype(jnp.int32), axis=0, keepdims=True)
        bounds += [below >> shift, (upto + (PAIR_GROUP - 1)) >> shift]
    pad = bnd_ref.shape[0] - len(bounds)
    bnd_ref[...] = jnp.concatenate(bounds + [jnp.zeros_like(bounds[0])] * pad, axis=0)


def _topk(x1b, wq, k1, k2, per_split):
    T, D = x1b.shape
    tt = TT_TOPK
    npair = PEER_HEADS * TOPK
    const = lambda shape: pl.BlockSpec(shape, lambda i: (0, 0))
    pair_major = pl.BlockSpec((npair, tt), lambda i: (0, i))
    token_major = pl.BlockSpec((tt, npair), lambda i: (i, 0))
    return pl.pallas_call(
        functools.partial(_topk_kernel, per_split),
        out_shape=(jax.ShapeDtypeStruct((npair, T), jnp.int32), jax.ShapeDtypeStruct((npair, T), F32),
                   jax.ShapeDtypeStruct((SUBLANES, T), jnp.int32))
        + tuple(jax.ShapeDtypeStruct((T, npair), jnp.int32) for _ in range(N_SPLIT)),
        grid=(T // tt,),
        in_specs=[pl.BlockSpec((tt, D), lambda i: (i, 0)), const(wq.shape), const(k1.shape), const(k2.shape)],
        out_specs=(pair_major, pair_major, pl.BlockSpec((SUBLANES, tt), lambda i: (0, i)))
        + tuple(token_major for _ in range(N_SPLIT)),
        compiler_params=_params(("parallel",)), name="topk",
    )(x1b, wq, k1, k2)


def _build_items(part, bnd_ref, items_ref, tt, groups):
    def per_token(t, n):
        first, last = bnd_ref[SUBLANES * t + 2 * part], bnd_ref[SUBLANES * t + 2 * part + 1]
        for g in range(groups):
            items_ref[n + g] = t * groups + first + g
        return n + (last - first)

    n = lax.fori_loop(0, tt, per_token, 0)
    final = items_ref[jnp.maximum(n - 1, 0)]
    for q in range(ITEM_UNROLL - 1):
        items_ref[n + q] = final
    return lax.shift_right_logical(n + (ITEM_UNROLL - 1), ITEM_UNROLL.bit_length() - 1)


def _pack_table(table, pieces):
    n, d = table.shape
    bits = lax.bitcast_convert_type(table.astype(BF16), jnp.uint16).astype(jnp.uint32)
    words = bits[:, :d // 2] | (bits[:, d // 2:] << 16)
    return words.reshape(pieces, (n // pieces) * SUBLANES, LANES)


def _table_row(tab_ref, row8):
    words = tab_ref[pl.ds(pl.multiple_of(row8, SUBLANES), SUBLANES), :]
    lo = pltpu.bitcast(words << 16, F32)
    hi = pltpu.bitcast(words & jnp.uint32(0xFFFF0000), F32)
    return lo, hi


def _rows_to_sublanes(parts):
    sub = lax.broadcasted_iota(jnp.int32, (SUBLANES, LANES), 0)
    bits = SUBLANES.bit_length() - 1
    parts = [parts[int(format(i, f"0{bits}b")[::-1], 2)] for i in range(SUBLANES)]
    step = SUBLANES // 2
    while len(parts) > 1:
        low = (sub & step) == 0
        nxt = []
        for a, b in zip(parts[0::2], parts[1::2]):
            if 2 * step == SUBLANES:
                nxt.append(jnp.where(low, a, b) + pltpu.roll(jnp.where(low, b, a), step, axis=0))
            else:
                t = jnp.where(low, a, pltpu.roll(b, step, axis=0))
                u = jnp.where(low, pltpu.roll(a, SUBLANES - step, axis=0), b)
                nxt.append(t + u)
        parts = nxt
        step //= 2
    return parts[0]


def _hside_kernel(part, bnd_ref, idx_ref, x_ref, u_ref, h_ref, items_ref, sums_ref):
    npair, tt = h_ref.shape
    groups = npair // PAIR_GROUP
    gshift = groups.bit_length() - 1
    nblk = _build_items(part, bnd_ref, items_ref, tt, groups)

    @pl.when(pl.program_id(0) == 0)
    def _():
        sums_ref[...] = jnp.zeros(sums_ref.shape, F32)

    def block(i, carry):
        for q in range(ITEM_UNROLL):
            it = items_ref[i * ITEM_UNROLL + q]
            t = lax.shift_right_logical(it, gshift)
            base = it * PAIR_GROUP
            xf = x_ref[t].astype(F32)
            x_lo, x_hi = xf[0:SUBLANES], xf[SUBLANES:2 * SUBLANES]
            for sub in range(PAIR_GROUP // SUBLANES):
                parts = []
                for r in range(SUBLANES):
                    u_lo, u_hi = _table_row(u_ref, idx_ref[base + sub * SUBLANES + r])
                    parts.append(x_lo * u_lo + x_hi * u_hi)
                row0 = pl.multiple_of(base + sub * SUBLANES, SUBLANES)
                sums_ref[pl.ds(row0, SUBLANES), :] = _rows_to_sublanes(parts)
        return carry

    lax.fori_loop(0, nblk, block, 0)

    lane = lax.broadcasted_iota(jnp.int32, (npair, tt), 1)
    h_ref[...] = jnp.zeros(h_ref.shape, F32)

    def finish(t, carry):
        rows = sums_ref[pl.ds(pl.multiple_of(t * npair, npair), npair), :]
        h_ref[...] = jnp.where(lane == t, jnp.sum(rows, axis=1, keepdims=True), h_ref[...])
        return carry

    lax.fori_loop(0, tt, finish, 0, unroll=8)


def _peer_specs(npair, tt):
    bnd = pl.BlockSpec((tt * SUBLANES,), lambda i: (i,), memory_space=pltpu.SMEM)
    idx = pl.BlockSpec((tt * npair,), lambda i: (i,), memory_space=pltpu.SMEM)
    return bnd, idx


def _items_scratch(npair, tt):
    return pltpu.SMEM((tt * (npair // PAIR_GROUP) + PAIR_GROUP + ITEM_UNROLL,), jnp.int32)


def _hside(part, bnd, idx, x1p, table, npair):
    T = x1p.shape[0]
    tt = TT_PEER
    return pl.pallas_call(
        functools.partial(_hside_kernel, part),
        out_shape=jax.ShapeDtypeStruct((npair, T), F32), grid=(T // tt,),
        in_specs=[*_peer_specs(npair, tt),
                  pl.BlockSpec((tt,) + x1p.shape[1:], lambda i: (i, 0, 0)),
                  pl.BlockSpec(memory_space=pltpu.VMEM)],
        out_specs=pl.BlockSpec((npair, tt), lambda i: (0, i)),
        scratch_shapes=[_items_scratch(npair, tt), pltpu.VMEM((tt * npair, LANES), F32)],
        compiler_params=_params(("arbitrary",)), name="hside",
    )(bnd, idx, x1p, table)


def _act_kernel(per_split, e_ref, g_ref, *refs):
    h_refs, a_refs = refs[:N_SPLIT], refs[N_SPLIT:]
    e = e_ref[...]
    in_part = [(e >= k * per_split) & (e < (k + 1) * per_split) for k in range(N_SPLIT)]
    h = h_refs[0][...]
    for k in range(1, N_SPLIT):
        h = jnp.where(in_part[k], h_refs[k][...], h)
    a = 0.5 * h * (1.0 + lax.erf(h * math.sqrt(0.5))) * g_ref[...]
    for k in range(N_SPLIT):
        a_refs[k][...] = jnp.where(in_part[k], a, 0.0).T


def _act(e, g, hs, per_split):
    npair, T = e.shape
    tt = min(1024, T)
    pair_major = pl.BlockSpec((npair, tt), lambda i: (0, i))
    token_major = pl.BlockSpec((tt, npair), lambda i: (i, 0))
    return pl.pallas_call(
        functools.partial(_act_kernel, per_split),
        out_shape=tuple(jax.ShapeDtypeStruct((T, npair), F32) for _ in range(N_SPLIT)),
        grid=(T // tt,), in_specs=[pair_major] * (2 + N_SPLIT),
        out_specs=tuple([token_major] * N_SPLIT),
        compiler_params=_params(("parallel",)), name="act",
    )(e, g, *hs)


def _yside_kernel(part, has_prev, npair, bnd_ref, idx_ref, a_ref, v_ref, *refs):
    y_ref, items_ref, part_ref = refs[-3], refs[-2], refs[-1]
    tt = y_ref.shape[0]
    groups = npair // PAIR_GROUP
    n_acc = 4
    nblk = _build_items(part, bnd_ref, items_ref, tt, groups)

    @pl.when(pl.program_id(0) == 0)
    def _():
        part_ref[...] = jnp.zeros(part_ref.shape, F32)

    def block(i, carry):
        for q in range(ITEM_UNROLL):
            it = items_ref[i * ITEM_UNROLL + q]
            base = it * PAIR_GROUP
            los, his = [None] * n_acc, [None] * n_acc
            for r in range(PAIR_GROUP):
                w = a_ref[base + r]
                v_lo, v_hi = _table_row(v_ref, idx_ref[base + r])
                k = r % n_acc
                los[k] = w * v_lo if los[k] is None else los[k] + w * v_lo
                his[k] = w * v_hi if his[k] is None else his[k] + w * v_hi
            lo = (los[0] + los[1]) + (los[2] + los[3])
            hi = (his[0] + his[1]) + (his[2] + his[3])
            part_ref[it] = jnp.concatenate([lo, hi], axis=0)
        return carry

    lax.fori_loop(0, nblk, block, 0)

    group_id = lax.broadcasted_iota(jnp.int32, (groups,) + part_ref.shape[1:], 0)

    def finish(t, carry):
        first, last = bnd_ref[SUBLANES * t + 2 * part], bnd_ref[SUBLANES * t + 2 * part + 1]
        sums = part_ref[pl.ds(pl.multiple_of(t * groups, groups), groups)]
        live = (group_id - first).astype(jnp.uint32) < (last - first).astype(jnp.uint32)
        y = jnp.sum(jnp.where(live, sums, 0.0), axis=0)
        if has_prev:
            y = y + refs[0][t]
        y_ref[t] = y
        return carry

    lax.fori_loop(0, tt, finish, 0, unroll=4)


def _yside(part, bnd, idx, a, table, prev, npair, packed):
    T = idx.shape[0] // npair
    tt = TT_PEER
    tile = pl.BlockSpec((tt,) + packed, lambda i: (i, 0, 0))
    bnd_spec, idx_spec = _peer_specs(npair, tt)
    in_specs = [bnd_spec, idx_spec, idx_spec, pl.BlockSpec(memory_space=pltpu.VMEM)]
    args = [bnd, idx, a, table]
    if prev is not None:
        in_specs.append(tile)
        args.append(prev)
    return pl.pallas_call(
        functools.partial(_yside_kernel, part, prev is not None, npair),
        out_shape=jax.ShapeDtypeStruct((T,) + packed, F32), grid=(T // tt,),
        in_specs=in_specs, out_specs=tile,
        scratch_shapes=[_items_scratch(npair, tt),
                        pltpu.VMEM((tt * (npair // PAIR_GROUP),) + packed, F32)],
        compiler_params=_params(("arbitrary",)), name="yside",
    )(*args)


def _ln2_kernel(alpha, x_ref, y_ref, g_ref, b_ref, o_ref):
    o_ref[...] = _layer_norm(alpha * x_ref[...] + y_ref[...], g_ref[...], b_ref[...])


def _ln2(x1, y, g, b, alpha):
    T, D = x1.shape
    tm = 512
    tile = pl.BlockSpec((tm, D), lambda i: (i, 0))
    vec = pl.BlockSpec((1, D), lambda i: (0, 0))
    return pl.pallas_call(
        functools.partial(_ln2_kernel, alpha), out_shape=jax.ShapeDtypeStruct((T, D), F32),
        grid=(T // tm,), in_specs=[tile, tile, vec, vec], out_specs=tile,
        compiler_params=_params(("parallel",)), name="ln2")(x1, y, g, b)


def _rope_tables(positions):
    T = positions.size
    half = ROT_DIM // 2
    inv_freq = ROPE_THETA ** (-jnp.arange(0, ROT_DIM, 2, dtype=F32) / ROT_DIM)
    ang = positions.reshape(T, 1).astype(F32) * inv_freq
    cos, sin = jnp.cos(ang), jnp.sin(ang)
    rest = HEAD_DIM - ROT_DIM
    zeros_h = jnp.zeros((T, half), F32)
    c = jnp.concatenate([cos, cos, jnp.ones((T, rest), F32)], axis=-1)
    s1 = jnp.concatenate([-sin, zeros_h, jnp.zeros((T, rest), F32)], axis=-1)
    s2 = jnp.concatenate([zeros_h, sin, jnp.zeros((T, rest), F32)], axis=-1)
    rep = LANES // HEAD_DIM
    return tuple(jnp.tile(t, (1, rep)) for t in (c, s1, s2))


def kernel(x, positions, w_in, b_in, conv_dw_w, conv_dw_b, conv_ln_g, conv_ln_b, w_conv_out, attn_sinks, w_mix_out, ln1_g, ln1_b, w_peer_q, sub_keys_1, sub_keys_2, expert_u, expert_v, ln2_g, ln2_b):
    B, S, D = x.shape
    T = B * S
    depth = w_in.shape[0]
    alpha = (2 * depth) ** 0.25
    dc = conv_dw_w.shape[-1]
    qw = N_HEADS * HEAD_DIM
    kvw = N_KV_HEADS * HEAD_DIM
    n_exp = expert_u.shape[1]
    per_split = n_exp // N_SPLIT
    packed = (D // LANES, LANES)
    c, s1, s2 = _rope_tables(positions)
    row = lambda v: v.reshape(1, -1)

    x2 = x.reshape(T, D)
    for l in range(depth):
        w = w_in[l].astype(BF16)
        b = b_in[l]
        o_q = 2 * dc
        o_k, o_v, o_ga = o_q + qw, o_q + qw + kvw, o_q + qw + 2 * kvw
        u = _glu(x2, w[:, :dc], w[:, dc:2 * dc], row(b[:dc]), row(b[dc:2 * dc]))
        qkv = _qkv(x2, w[:, o_q:o_ga], row(b[o_q:o_ga]), c, s1, s2, qw + kvw)
        gv = _gates(x2, w[:, o_ga:], row(b[o_ga:]))
        ya = _attention(qkv, attn_sinks[l], B, S)
        x1, x1b = _mix(u, conv_dw_w[l].reshape(-1, dc), row(conv_dw_b[l]), row(conv_ln_g[l]),
                       row(conv_ln_b[l]), w_conv_out[l].astype(BF16), gv, ya,
                       w_mix_out[l].astype(BF16), x2, row(ln1_g[l]), row(ln1_b[l]), alpha, B, S)
        outs = _topk(x1b, w_peer_q[l].astype(BF16), sub_keys_1[l].astype(BF16),
                     sub_keys_2[l].astype(BF16), per_split)
        e, g = outs[0], outs[1]
        npair = e.shape[0]
        bnd = outs[2].T.reshape(-1)
        idxs = [i.reshape(-1) for i in outs[3:]]
        x1p = x1b.reshape((T,) + packed)
        ut = _pack_table(expert_u[l], N_SPLIT)
        vt = _pack_table(expert_v[l], N_SPLIT)
        hs = [_hside(k, bnd, idxs[k], x1p, ut[k], npair) for k in range(N_SPLIT)]
        acts = _act(e, g, hs, per_split)
        y = None
        for k in range(N_SPLIT):
            y = _yside(k, bnd, idxs[k], acts[k].reshape(-1), vt[k], y, npair, packed)
        x2 = _ln2(x1, y.reshape(T, D), row(ln2_g[l]), row(ln2_b[l]), alpha)
    return x2.reshape(B, S, D)
```

```python
import functools
import math

import jax
import jax.numpy as jnp
from jax import lax
from jax.experimental import pallas as pl
from jax.experimental.pallas import tpu as pltpu

F32 = jnp.float32
BF16 = jnp.bfloat16

N_HEADS = 32
N_KV_HEADS = 4
GROUP = N_HEADS // N_KV_HEADS
HEAD_DIM = 64
WINDOW = 128
ROPE_THETA = 500000.0
ROT_DIM = HEAD_DIM // 4
ATTN_SCALE = 1.0 / math.sqrt(HEAD_DIM)
PEER_HEADS = 8
N_KEYS = 128
TOPK = 16
LN_EPS = 1e-5

LANES = 128
SUBLANES = 8
VMEM_LIMIT = 56 * 1024 * 1024

TM_PROJ = 1024
TN_PROJ = 512
TS_MIX = 256
HALO = 32
TT_TOPK = 256
TT_PEER = 128
N_SPLIT = 2
PAIR_GROUP = 16
ITEM_UNROLL = 8


def _params(sem):
    return pltpu.CompilerParams(dimension_semantics=sem, vmem_limit_bytes=VMEM_LIMIT)


def _glu_kernel(x_ref, wa_ref, wb_ref, ba_ref, bb_ref, o_ref, xb_ref):
    @pl.when(pl.program_id(1) == 0)
    def _():
        xb_ref[...] = x_ref[...].astype(BF16)

    xb = xb_ref[...]
    za = jnp.dot(xb, wa_ref[...], preferred_element_type=F32) + ba_ref[...]
    zb = jnp.dot(xb, wb_ref[...], preferred_element_type=F32) + bb_ref[...]
    o_ref[...] = za * jax.nn.sigmoid(zb)


def _rope_block(z, c, s1, s2):
    half = ROT_DIM // 2
    return z * c + pltpu.roll(z, LANES - half, axis=1) * s1 + pltpu.roll(z, half, axis=1) * s2


def _qkv_kernel(n_rope_cols, x_ref, w_ref, b_ref, c_ref, s1_ref, s2_ref, o_ref, xb_ref):
    j = pl.program_id(1)

    @pl.when(j == 0)
    def _():
        xb_ref[...] = x_ref[...].astype(BF16)

    z = jnp.dot(xb_ref[...], w_ref[...], preferred_element_type=F32) + b_ref[...]
    c, s1, s2 = c_ref[...], s1_ref[...], s2_ref[...]
    tn = z.shape[1]
    for k in range(tn // LANES):
        sl = slice(LANES * k, LANES * (k + 1))
        rotary = j * tn + LANES * k < n_rope_cols
        o_ref[:, sl] = jnp.where(rotary, _rope_block(z[:, sl], c, s1, s2), z[:, sl]).astype(o_ref.dtype)


def _gates_kernel(x_ref, w_ref, b_ref, o_ref, xb_ref):
    @pl.when(pl.program_id(1) == 0)
    def _():
        xb_ref[...] = x_ref[...].astype(BF16)

    z = jnp.dot(xb_ref[...], w_ref[...], preferred_element_type=F32) + b_ref[...]
    o_ref[...] = jax.nn.sigmoid(z).astype(o_ref.dtype)


def _proj_specs(T, D, N, n_w, extra_specs=()):
    tm, tn = TM_PROJ, TN_PROJ
    assert T % tm == 0 and N % tn == 0, (T, N)
    grid = (T // tm, N // tn)
    in_specs = [pl.BlockSpec((tm, D), lambda i, j: (i, 0))]
    in_specs += [pl.BlockSpec((D, tn), lambda i, j: (0, j)) for _ in range(n_w)]
    in_specs += [pl.BlockSpec((1, tn), lambda i, j: (0, j)) for _ in range(n_w)]
    in_specs += list(extra_specs)
    out_spec = pl.BlockSpec((tm, tn), lambda i, j: (i, j))
    scratch = [pltpu.VMEM((tm, D), BF16)]
    return grid, in_specs, out_spec, scratch


def _glu(x2, wa, wb, ba, bb):
    T, D = x2.shape
    N = wa.shape[1]
    grid, in_specs, out_spec, scratch = _proj_specs(T, D, N, 2)
    return pl.pallas_call(
        _glu_kernel, out_shape=jax.ShapeDtypeStruct((T, N), F32), grid=grid,
        in_specs=in_specs, out_specs=out_spec, scratch_shapes=scratch,
        compiler_params=_params(("parallel", "arbitrary")), name="glu")(x2, wa, wb, ba, bb)


def _qkv(x2, w, b, c, s1, s2, n_rope_cols):
    T, D = x2.shape
    N = w.shape[1]
    tab = pl.BlockSpec((TM_PROJ, LANES), lambda i, j: (i, 0))
    grid, in_specs, out_spec, scratch = _proj_specs(T, D, N, 1, (tab, tab, tab))
    return pl.pallas_call(
        functools.partial(_qkv_kernel, n_rope_cols),
        out_shape=jax.ShapeDtypeStruct((T, N), BF16), grid=grid,
        in_specs=in_specs, out_specs=out_spec, scratch_shapes=scratch,
        compiler_params=_params(("parallel", "arbitrary")), name="qkv")(x2, w, b, c, s1, s2)


def _gates(x2, w, b):
    T, D = x2.shape
    N = w.shape[1]
    grid, in_specs, out_spec, scratch = _proj_specs(T, D, N, 1)
    return pl.pallas_call(
        _gates_kernel, out_shape=jax.ShapeDtypeStruct((T, N), BF16), grid=grid,
        in_specs=in_specs, out_specs=out_spec, scratch_shapes=scratch,
        compiler_params=_params(("parallel", "arbitrary")), name="gates")(x2, w, b)


def _attn_kernel(sink_ref, q_ref, kc_ref, kp_ref, vc_ref, vp_ref, o_ref):
    n = pl.program_id(1)
    blk = WINDOW
    lane = lax.broadcasted_iota(jnp.int32, (blk, LANES), 1)
    lo = lane < HEAD_DIM
    lane2 = lax.broadcasted_iota(jnp.int32, (2 * blk, LANES), 1)
    lo2 = lane2 < HEAD_DIM
    qi = lax.broadcasted_iota(jnp.int32, (blk, 2 * blk), 0)
    sj = lax.broadcasted_iota(jnp.int32, (blk, 2 * blk), 1)
    valid = (sj > qi) & (sj <= qi + blk) & (sj >= blk * (1 - n))
    neg = jnp.finfo(F32).min
    kk = jnp.concatenate([kp_ref[...], kc_ref[...]], axis=0).astype(F32)
    vv = jnp.concatenate([vp_ref[...], vc_ref[...]], axis=0).astype(F32)
    heads_per_tile = LANES // HEAD_DIM
    tiles_per_group = GROUP // heads_per_tile
    for g in range(N_KV_HEADS):
        tsl = slice(LANES * (g // heads_per_tile), LANES * (g // heads_per_tile + 1))

        def both_halves(t):
            r = pltpu.roll(t, HEAD_DIM, axis=1)
            return (jnp.where(lo2, t, r) if g % heads_per_tile == 0 else jnp.where(lo2, r, t)).astype(BF16)

        kd = both_halves(kk[:, tsl])
        vd = both_halves(vv[:, tsl])
        qs = []
        for c in range(tiles_per_group):
            col = LANES * (tiles_per_group * g + c)
            qt = q_ref[:, col:col + LANES]
            zero = jnp.zeros_like(qt)
            qs.append(jnp.where(lo, qt, zero))
            qs.append(jnp.where(lo, zero, qt))
        q8 = jnp.concatenate(qs, axis=0)
        s = lax.dot_general(q8, kd, (((1,), (1,)), ((), ())), preferred_element_type=F32) * ATTN_SCALE
        s3 = jnp.where(valid[None], s.reshape(GROUP, blk, 2 * blk), neg)
        sink = sink_ref[GROUP * g:GROUP * (g + 1)]
        m = jnp.maximum(jnp.max(s3, axis=-1, keepdims=True), sink)
        p = jnp.exp(s3 - m)
        denom = jnp.sum(p, axis=-1, keepdims=True) + jnp.exp(sink - m)
        p = p * (1.0 / denom)
        o = jnp.dot(p.reshape(GROUP * blk, 2 * blk).astype(BF16), vd, preferred_element_type=F32)
        for c in range(tiles_per_group):
            oa = o[blk * (2 * c):blk * (2 * c + 1)]
            ob = o[blk * (2 * c + 1):blk * (2 * c + 2)]
            col = LANES * (tiles_per_group * g + c)
            o_ref[:, col:col + LANES] = jnp.where(lo, oa, ob).astype(o_ref.dtype)


def _attention(qkv, sinks, B, S):
    T = B * S
    blk = WINDOW
    nb = S // blk
    qw = N_HEADS * HEAD_DIM
    kvw = N_KV_HEADS * HEAD_DIM
    k_col = qw // kvw
    v_col = k_col + 1
    row = lambda b, n: b * nb + n
    prev = lambda b, n: jnp.maximum(b * nb + n - 1, 0)
    in_specs = [
        pl.BlockSpec((N_HEADS, 1, 1), lambda b, n: (0, 0, 0)),
        pl.BlockSpec((blk, qw), lambda b, n: (row(b, n), 0)),
        pl.BlockSpec((blk, kvw), lambda b, n: (row(b, n), k_col)),
        pl.BlockSpec((blk, kvw), lambda b, n: (prev(b, n), k_col)),
        pl.BlockSpec((blk, kvw), lambda b, n: (row(b, n), v_col)),
        pl.BlockSpec((blk, kvw), lambda b, n: (prev(b, n), v_col)),
    ]
    return pl.pallas_call(
        _attn_kernel, out_shape=jax.ShapeDtypeStruct((T, qw), BF16), grid=(B, nb),
        in_specs=in_specs, out_specs=pl.BlockSpec((blk, qw), lambda b, n: (row(b, n), 0)),
        compiler_params=_params(("parallel", "arbitrary")), name="attn",
    )(sinks.astype(F32).reshape(N_HEADS, 1, 1), qkv, qkv, qkv, qkv, qkv)


def _layer_norm(x, g, b):
    mu = jnp.mean(x, axis=-1, keepdims=True)
    xc = x - mu
    var = jnp.mean(xc * xc, axis=-1, keepdims=True)
    return xc * lax.rsqrt(var + LN_EPS) * g + b


def _mix_kernel(alpha, u_ref, uh_ref, cw_ref, cb_ref, cg_ref, cbeta_ref, wco_ref, sa_ref, sb_ref,
                ya_ref, wmo_ref, x_ref, g1_ref, b1_ref, x1_ref, x1b_ref, win_ref, conv_ref, shift_ref):
    s = pl.program_id(1)
    ts, dc = u_ref.shape
    width = cw_ref.shape[0]
    win_ref[0:HALO, :] = jnp.where(s == 0, 0.0, uh_ref[...])
    win_ref[HALO:HALO + ts, :] = u_ref[...]
    first = HALO - (width - 1)
    span = ts + HALO - SUBLANES
    for c in range(dc // LANES):
        cs = slice(LANES * c, LANES * (c + 1))
        for r in range(1, SUBLANES):
            shift_ref[r, 0:span, :] = win_ref[r:r + span, cs]
        acc = jnp.broadcast_to(cb_ref[:, cs], (ts, LANES))
        for j in range(width):
            r, a = (first + j) % SUBLANES, (first + j) // SUBLANES * SUBLANES
            rows = win_ref[a:a + ts, cs] if r == 0 else shift_ref[r, a:a + ts, :]
            acc = acc + cw_ref[j:j + 1, cs] * rows
        conv_ref[:, cs] = acc
    h = _layer_norm(conv_ref[...], cg_ref[...], cbeta_ref[...])
    h = h * jax.nn.sigmoid(h)
    yc = jnp.dot(h.astype(BF16), wco_ref[...], preferred_element_type=F32)
    m = sa_ref[...].astype(F32) * yc + sb_ref[...].astype(F32) * ya_ref[...].astype(F32)
    mix = jnp.dot(m.astype(BF16), wmo_ref[...], preferred_element_type=F32)
    x1 = _layer_norm(alpha * x_ref[...] + mix, g1_ref[...], b1_ref[...])
    x1_ref[...] = x1
    x1b_ref[...] = x1.astype(BF16)


def _mix(u, cw, cb, cg, cbeta, wco, gv, ya, wmo, x2, g1, b1, alpha, B, S):
    T, D = x2.shape
    dc = u.shape[1]
    ts = TS_MIX
    ns = S // ts
    row = lambda b, s: b * ns + s
    halo = lambda b, s: jnp.maximum((b * S + s * ts) // HALO - 1, 0)
    const = lambda shape: pl.BlockSpec(shape, lambda b, s: (0, 0))
    in_specs = [
        pl.BlockSpec((ts, dc), lambda b, s: (row(b, s), 0)),
        pl.BlockSpec((HALO, dc), lambda b, s: (halo(b, s), 0)),
        const(cw.shape), const((1, dc)), const((1, dc)), const((1, dc)),
        const(wco.shape),
        pl.BlockSpec((ts, D), lambda b, s: (row(b, s), 0)),
        pl.BlockSpec((ts, D), lambda b, s: (row(b, s), 1)),
        pl.BlockSpec((ts, D), lambda b, s: (row(b, s), 0)),
        const(wmo.shape),
        pl.BlockSpec((ts, D), lambda b, s: (row(b, s), 0)),
        const((1, D)), const((1, D)),
    ]
    out_spec = pl.BlockSpec((ts, D), lambda b, s: (row(b, s), 0))
    return pl.pallas_call(
        functools.partial(_mix_kernel, alpha),
        out_shape=(jax.ShapeDtypeStruct((T, D), F32), jax.ShapeDtypeStruct((T, D), BF16)),
        grid=(B, ns), in_specs=in_specs, out_specs=(out_spec, out_spec),
        scratch_shapes=[pltpu.VMEM((HALO + ts, dc), F32), pltpu.VMEM((ts, dc), F32),
                        pltpu.VMEM((SUBLANES, HALO + ts, LANES), F32)],
        compiler_params=_params(("parallel", "arbitrary")), name="mix",
    )(u, u, cw, cb, cg, cbeta, wco, gv, gv, ya, wmo, x2, g1, b1)


def _topk_axis0(s, k, payload=None, rank=None):
    iota = lax.broadcasted_iota(jnp.int32, s.shape, 0) if rank is None else rank
    big = jnp.iinfo(jnp.int32).max
    vals, outs = [], []
    for _ in range(k):
        m = jnp.max(s, axis=0, keepdims=True)
        idx = jnp.min(jnp.where(s == m, iota, big), axis=0, keepdims=True)
        sel = iota == idx
        vals.append(m)
        if payload is None:
            outs.append(idx)
        else:
            outs.append(jnp.max(jnp.where(sel, payload, -1), axis=0, keepdims=True))
        s = jnp.where(sel, -jnp.inf, s)
    return jnp.concatenate(vals, axis=0), jnp.concatenate(outs, axis=0)


def _candidate_grid(v1, i1, v2, i2):
    k, t = v1.shape
    sub = lax.broadcasted_iota(jnp.int32, (SUBLANES, t), 0)
    vals, ids, flats = [], [], []
    for a in range(k // 2):
        nb = k // (a + 1)
        for b0 in range(0, nb, SUBLANES):
            live = sub < (nb - b0)
            vals.append(jnp.where(live, v1[a:a + 1] + v2[b0:b0 + SUBLANES], -jnp.inf))
            ids.append(i1[a:a + 1] * N_KEYS + i2[b0:b0 + SUBLANES])
            flats.append(a * k + b0 + sub)
    for a0 in range(k // 2, k, SUBLANES):
        vals.append(v1[a0:a0 + SUBLANES] + v2[0:1])
        ids.append(i1[a0:a0 + SUBLANES] * N_KEYS + i2[0:1])
        flats.append((a0 + sub) * k)
    return jnp.concatenate(vals, axis=0), jnp.concatenate(ids, axis=0), jnp.concatenate(flats, axis=0)


def _partner_rows(x, j):
    n = x.shape[0]
    if j >= SUBLANES:
        return jnp.concatenate([x[(b ^ 1) * j:((b ^ 1) + 1) * j] for b in range(n // j)], axis=0)
    sub = lax.broadcasted_iota(jnp.int32, (SUBLANES, x.shape[1]), 0)
    low = (sub & j) == 0
    out = []
    for r in range(n // SUBLANES):
        slab = x[SUBLANES * r:SUBLANES * (r + 1)]
        out.append(jnp.where(low, pltpu.roll(slab, SUBLANES - j, axis=0), pltpu.roll(slab, j, axis=0)))
    return jnp.concatenate(out, axis=0)


def _sort_axis0(key, val):
    n = key.shape[0]
    row = lax.broadcasted_iota(jnp.int32, key.shape, 0)
    k = 2
    while k <= n:
        j = k // 2
        while j >= 1:
            pk, pv = _partner_rows(key, j), _partner_rows(val, j)
            sign = 1 - 2 * (((row & j) != 0).astype(jnp.int32) ^ ((row & k) != 0).astype(jnp.int32))
            take = (pk - key) * sign < 0
            key = jnp.where(take, pk, key)
            val = jnp.where(take, pv, val)
            j //= 2
        k *= 2
    return key, val


def _topk_kernel(per_split, x_ref, wq_ref, k1_ref, k2_ref, e_ref, g_ref, bnd_ref, *idx_refs):
    q = jnp.dot(x_ref[...], wq_ref[...], preferred_element_type=F32).astype(BF16)
    nt = (((1,), (1,)), ((), ()))
    half = k1_ref.shape[1]
    es, gs = [], []
    for h in range(PEER_HEADS):
        q1 = q[:, 2 * half * h:2 * half * h + half]
        q2 = q[:, 2 * half * h + half:2 * half * (h + 1)]
        s1 = lax.dot_general(k1_ref[...], q1, nt, preferred_element_type=F32)
        s2 = lax.dot_general(k2_ref[...], q2, nt, preferred_element_type=F32)
        v1, i1 = _topk_axis0(s1, TOPK)
        v2, i2 = _topk_axis0(s2, TOPK)
        cand, cidx, flat = _candidate_grid(v1, i1, v2, i2)
        best, eidx = _topk_axis0(cand, TOPK, payload=cidx, rank=flat)
        w = jnp.exp(best - best[0:1])
        es.append(eidx)
        gs.append(w / jnp.sum(w, axis=0, keepdims=True))
    e, g = _sort_axis0(jnp.concatenate(es, axis=0), jnp.concatenate(gs, axis=0))
    e_ref[...] = e
    g_ref[...] = g
    shift = PAIR_GROUP.bit_length() - 1
    bounds = []
    for part, idx_ref in enumerate(idx_refs):
        idx_ref[...] = (jnp.clip(e - part * per_split, 0, per_split - 1) * SUBLANES).T
        below = jnp.sum((e < part * per_split).astype(jnp.int32), axis=0, keepdims=True)
        upto = jnp.sum((e < (part + 1) * per_split).astype(jnp.int32), axis=0, keepdims=True)
        bounds += [below >> shift, (upto + (PAIR_GROUP - 1)) >> shift]
    pad = bnd_ref.shape[0] - len(bounds)
    bnd_ref[...] = jnp.concatenate(bounds + [jnp.zeros_like(bounds[0])] * pad, axis=0)


def _topk(x1b, wq, k1, k2, per_split):
    T, D = x1b.shape
    tt = TT_TOPK
    npair = PEER_HEADS * TOPK
    const = lambda shape: pl.BlockSpec(shape, lambda i: (0, 0))
    pair_major = pl.BlockSpec((npair, tt), lambda i: (0, i))
    token_major = pl.BlockSpec((tt, npair), lambda i: (i, 0))
    return pl.pallas_call(
        functools.partial(_topk_kernel, per_split),
        out_shape=(jax.ShapeDtypeStruct((npair, T), jnp.int32), jax.ShapeDtypeStruct((npair, T), F32),
                   jax.ShapeDtypeStruct((SUBLANES, T), jnp.int32))
        + tuple(jax.ShapeDtypeStruct((T, npair), jnp.int32) for _ in range(N_SPLIT)),
        grid=(T // tt,),
        in_specs=[pl.BlockSpec((tt, D), lambda i: (i, 0)), const(wq.shape), const(k1.shape), const(k2.shape)],
        out_specs=(pair_major, pair_major, pl.BlockSpec((SUBLANES, tt), lambda i: (0, i)))
        + tuple(token_major for _ in range(N_SPLIT)),
        compiler_params=_params(("parallel",)), name="topk",
    )(x1b, wq, k1, k2)


def _build_items(part, bnd_ref, items_ref, tt, groups):
    def per_token(t, n):
        first, last = bnd_ref[SUBLANES * t + 2 * part], bnd_ref[SUBLANES * t + 2 * part + 1]
        for g in range(groups):
            items_ref[n + g] = t * groups + first + g
        return n + (last - first)

    n = lax.fori_loop(0, tt, per_token, 0)
    final = items_ref[jnp.maximum(n - 1, 0)]
    for q in range(ITEM_UNROLL - 1):
        items_ref[n + q] = final
    return lax.shift_right_logical(n + (ITEM_UNROLL - 1), ITEM_UNROLL.bit_length() - 1)


def _pack_table(table, pieces):
    n, d = table.shape
    bits = lax.bitcast_convert_type(table.astype(BF16), jnp.uint16).astype(jnp.uint32)
    words = bits[:, :d // 2] | (bits[:, d // 2:] << 16)
    return words.reshape(pieces, (n // pieces) * SUBLANES, LANES)


def _table_row(tab_ref, row8):
    words = tab_ref[pl.ds(pl.multiple_of(row8, SUBLANES), SUBLANES), :]
    lo = pltpu.bitcast(words << 16, F32)
    hi = pltpu.bitcast(words & jnp.uint32(0xFFFF0000), F32)
    return lo, hi


def _rows_to_sublanes(parts):
    sub = lax.broadcasted_iota(jnp.int32, (SUBLANES, LANES), 0)
    bits = SUBLANES.bit_length() - 1
    parts = [parts[int(format(i, f"0{bits}b")[::-1], 2)] for i in range(SUBLANES)]
    step = SUBLANES // 2
    while len(parts) > 1:
        low = (sub & step) == 0
        nxt = []
        for a, b in zip(parts[0::2], parts[1::2]):
            if 2 * step == SUBLANES:
                nxt.append(jnp.where(low, a, b) + pltpu.roll(jnp.where(low, b, a), step, axis=0))
            else:
                t = jnp.where(low, a, pltpu.roll(b, step, axis=0))
                u = jnp.where(low, pltpu.roll(a, SUBLANES - step, axis=0), b)
                nxt.append(t + u)
        parts = nxt
        step //= 2
    return parts[0]


def _hside_kernel(part, bnd_ref, idx_ref, x_ref, u_ref, h_ref, items_ref, sums_ref):
    npair, tt = h_ref.shape
    groups = npair // PAIR_GROUP
    gshift = groups.bit_length() - 1
    nblk = _build_items(part, bnd_ref, items_ref, tt, groups)

    @pl.when(pl.program_id(0) == 0)
    def _():
        sums_ref[...] = jnp.zeros(sums_ref.shape, F32)

    def block(i, carry):
        for q in range(ITEM_UNROLL):
            it = items_ref[i * ITEM_UNROLL + q]
            t = lax.shift_right_logical(it, gshift)
            base = it * PAIR_GROUP
            xf = x_ref[t].astype(F32)
            x_lo, x_hi = xf[0:SUBLANES], xf[SUBLANES:2 * SUBLANES]
            for sub in range(PAIR_GROUP // SUBLANES):
                parts = []
                for r in range(SUBLANES):
                    u_lo, u_hi = _table_row(u_ref, idx_ref[base + sub * SUBLANES + r])
                    parts.append(x_lo * u_lo + x_hi * u_hi)
                row0 = pl.multiple_of(base + sub * SUBLANES, SUBLANES)
                sums_ref[pl.ds(row0, SUBLANES), :] = _rows_to_sublanes(parts)
        return carry

    lax.fori_loop(0, nblk, block, 0)

    lane = lax.broadcasted_iota(jnp.int32, (npair, tt), 1)
    h_ref[...] = jnp.zeros(h_ref.shape, F32)

    def finish(t, carry):
        rows = sums_ref[pl.ds(pl.multiple_of(t * npair, npair), npair), :]
        h_ref[...] = jnp.where(lane == t, jnp.sum(rows, axis=1, keepdims=True), h_ref[...])
        return carry

    lax.fori_loop(0, tt, finish, 0, unroll=16)


def _peer_specs(npair, tt):
    bnd = pl.BlockSpec((tt * SUBLANES,), lambda i: (i,), memory_space=pltpu.SMEM)
    idx = pl.BlockSpec((tt * npair,), lambda i: (i,), memory_space=pltpu.SMEM)
    return bnd, idx


def _items_scratch(npair, tt):
    return pltpu.SMEM((tt * (npair // PAIR_GROUP) + PAIR_GROUP + ITEM_UNROLL,), jnp.int32)


def _hside(part, bnd, idx, x1p, table, npair):
    T = x1p.shape[0]
    tt = TT_PEER
    return pl.pallas_call(
        functools.partial(_hside_kernel, part),
        out_shape=jax.ShapeDtypeStruct((npair, T), F32), grid=(T // tt,),
        in_specs=[*_peer_specs(npair, tt),
                  pl.BlockSpec((tt,) + x1p.shape[1:], lambda i: (i, 0, 0)),
                  pl.BlockSpec(memory_space=pltpu.VMEM)],
        out_specs=pl.BlockSpec((npair, tt), lambda i: (0, i)),
        scratch_shapes=[_items_scratch(npair, tt), pltpu.VMEM((tt * npair, LANES), F32)],
        compiler_params=_params(("arbitrary",)), name="hside",
    )(bnd, idx, x1p, table)


def _act_kernel(per_split, e_ref, g_ref, *refs):
    h_refs, a_refs = refs[:N_SPLIT], refs[N_SPLIT:]
    e = e_ref[...]
    in_part = [(e >= k * per_split) & (e < (k + 1) * per_split) for k in range(N_SPLIT)]
    h = h_refs[0][...]
    for k in range(1, N_SPLIT):
        h = jnp.where(in_part[k], h_refs[k][...], h)
    a = 0.5 * h * (1.0 + lax.erf(h * math.sqrt(0.5))) * g_ref[...]
    for k in range(N_SPLIT):
        a_refs[k][...] = jnp.where(in_part[k], a, 0.0).T


def _act(e, g, hs, per_split):
    npair, T = e.shape
    tt = min(1024, T)
    pair_major = pl.BlockSpec((npair, tt), lambda i: (0, i))
    token_major = pl.BlockSpec((tt, npair), lambda i: (i, 0))
    return pl.pallas_call(
        functools.partial(_act_kernel, per_split),
        out_shape=tuple(jax.ShapeDtypeStruct((T, npair), F32) for _ in range(N_SPLIT)),
        grid=(T // tt,), in_specs=[pair_major] * (2 + N_SPLIT),
        out_specs=tuple([token_major] * N_SPLIT),
        compiler_params=_params(("parallel",)), name="act",
    )(e, g, *hs)


def _yside_kernel(part, has_prev, npair, bnd_ref, idx_ref, a_ref, v_ref, *refs):
    y_ref, items_ref, part_ref = refs[-3], refs[-2], refs[-1]
    tt = y_ref.shape[0]
    groups = npair // PAIR_GROUP
    n_acc = 4
    nblk = _build_items(part, bnd_ref, items_ref, tt, groups)

    @pl.when(pl.program_id(0) == 0)
    def _():
        part_ref[...] = jnp.zeros(part_ref.shape, F32)

    def block(i, carry):
        for q in range(ITEM_UNROLL):
            it = items_ref[i * ITEM_UNROLL + q]
            base = it * PAIR_GROUP
            los, his = [None] * n_acc, [None] * n_acc
            for r in range(PAIR_GROUP):
                w = a_ref[base + r]
                v_lo, v_hi = _table_row(v_ref, idx_ref[base + r])
                k = r % n_acc
                los[k] = w * v_lo if los[k] is None else los[k] + w * v_lo
                his[k] = w * v_hi if his[k] is None else his[k] + w * v_hi
            lo = (los[0] + los[1]) + (los[2] + los[3])
            hi = (his[0] + his[1]) + (his[2] + his[3])
            part_ref[it] = jnp.concatenate([lo, hi], axis=0)
        return carry

    lax.fori_loop(0, nblk, block, 0)

    group_id = lax.broadcasted_iota(jnp.int32, (groups,) + part_ref.shape[1:], 0)

    def finish(t, carry):
        first, last = bnd_ref[SUBLANES * t + 2 * part], bnd_ref[SUBLANES * t + 2 * part + 1]
        sums = part_ref[pl.ds(pl.multiple_of(t * groups, groups), groups)]
        live = (group_id - first).astype(jnp.uint32) < (last - first).astype(jnp.uint32)
        y = jnp.sum(jnp.where(live, sums, 0.0), axis=0)
        if has_prev:
            y = y + refs[0][t]
        y_ref[t] = y
        return carry

    lax.fori_loop(0, tt, finish, 0, unroll=8)


def _yside(part, bnd, idx, a, table, prev, npair, packed):
    T = idx.shape[0] // npair
    tt = TT_PEER
    tile = pl.BlockSpec((tt,) + packed, lambda i: (i, 0, 0))
    bnd_spec, idx_spec = _peer_specs(npair, tt)
    in_specs = [bnd_spec, idx_spec, idx_spec, pl.BlockSpec(memory_space=pltpu.VMEM)]
    args = [bnd, idx, a, table]
    if prev is not None:
        in_specs.append(tile)
        args.append(prev)
    return pl.pallas_call(
        functools.partial(_yside_kernel, part, prev is not None, npair),
        out_shape=jax.ShapeDtypeStruct((T,) + packed, F32), grid=(T // tt,),
        in_specs=in_specs, out_specs=tile,
        scratch_shapes=[_items_scratch(npair, tt),
                        pltpu.VMEM((tt * (npair // PAIR_GROUP),) + packed, F32)],
        compiler_params=_params(("arbitrary",)), name="yside",
    )(*args)


def _ln2_kernel(alpha, x_ref, y_ref, g_ref, b_ref, o_ref):
    o_ref[...] = _layer_norm(alpha * x_ref[...] + y_ref[...], g_ref[...], b_ref[...])


def _ln2(x1, y, g, b, alpha):
    T, D = x1.shape
    tm = 512
    tile = pl.BlockSpec((tm, D), lambda i: (i, 0))
    vec = pl.BlockSpec((1, D), lambda i: (0, 0))
    return pl.pallas_call(
        functools.partial(_ln2_kernel, alpha), out_shape=jax.ShapeDtypeStruct((T, D), F32),
        grid=(T // tm,), in_specs=[tile, tile, vec, vec], out_specs=tile,
        compiler_params=_params(("parallel",)), name="ln2")(x1, y, g, b)


def _rope_tables(positions):
    T = positions.size
    half = ROT_DIM // 2
    inv_freq = ROPE_THETA ** (-jnp.arange(0, ROT_DIM, 2, dtype=F32) / ROT_DIM)
    ang = positions.reshape(T, 1).astype(F32) * inv_freq
    cos, sin = jnp.cos(ang), jnp.sin(ang)
    rest = HEAD_DIM - ROT_DIM
    zeros_h = jnp.zeros((T, half), F32)
    c = jnp.concatenate([cos, cos, jnp.ones((T, rest), F32)], axis=-1)
    s1 = jnp.concatenate([-sin, zeros_h, jnp.zeros((T, rest), F32)], axis=-1)
    s2 = jnp.concatenate([zeros_h, sin, jnp.zeros((T, rest), F32)], axis=-1)
    rep = LANES // HEAD_DIM
    return tuple(jnp.tile(t, (1, rep)) for t in (c, s1, s2))


def kernel(x, positions, w_in, b_in, conv_dw_w, conv_dw_b, conv_ln_g, conv_ln_b, w_conv_out, attn_sinks, w_mix_out, ln1_g, ln1_b, w_peer_q, sub_keys_1, sub_keys_2, expert_u, expert_v, ln2_g, ln2_b):
    B, S, D = x.shape
    T = B * S
    depth = w_in.shape[0]
    alpha = (2 * depth) ** 0.25
    dc = conv_dw_w.shape[-1]
    qw = N_HEADS * HEAD_DIM
    kvw = N_KV_HEADS * HEAD_DIM
    n_exp = expert_u.shape[1]
    per_split = n_exp // N_SPLIT
    packed = (D // LANES, LANES)
    c, s1, s2 = _rope_tables(positions)
    row = lambda v: v.reshape(1, -1)

    x2 = x.reshape(T, D)
    for l in range(depth):
        w = w_in[l].astype(BF16)
        b = b_in[l]
        o_q = 2 * dc
        o_k, o_v, o_ga = o_q + qw, o_q + qw + kvw, o_q + qw + 2 * kvw
        u = _glu(x2, w[:, :dc], w[:, dc:2 * dc], row(b[:dc]), row(b[dc:2 * dc]))
        qkv = _qkv(x2, w[:, o_q:o_ga], row(b[o_q:o_ga]), c, s1, s2, qw + kvw)
        gv = _gates(x2, w[:, o_ga:], row(b[o_ga:]))
        ya = _attention(qkv, attn_sinks[l], B, S)
        x1, x1b = _mix(u, conv_dw_w[l].reshape(-1, dc), row(conv_dw_b[l]), row(conv_ln_g[l]),
                       row(conv_ln_b[l]), w_conv_out[l].astype(BF16), gv, ya,
                       w_mix_out[l].astype(BF16), x2, row(ln1_g[l]), row(ln1_b[l]), alpha, B, S)
        outs = _topk(x1b, w_peer_q[l].astype(BF16), sub_keys_1[l].astype(BF16),
                     sub_keys_2[l].astype(BF16), per_split)
        e, g = outs[0], outs[1]
        npair = e.shape[0]
        bnd = outs[2].T.reshape(-1)
        idxs = [i.reshape(-1) for i in outs[3:]]
        x1p = x1b.reshape((T,) + packed)
        ut = _pack_table(expert_u[l], N_SPLIT)
        vt = _pack_table(expert_v[l], N_SPLIT)
        hs = [_hside(k, bnd, idxs[k], x1p, ut[k], npair) for k in range(N_SPLIT)]
        acts = _act(e, g, hs, per_split)
        y = None
        for k in range(N_SPLIT):
            y = _yside(k, bnd, idxs[k], acts[k].reshape(-1), vt[k], y, npair, packed)
        x2 = _ln2(x1, y.reshape(T, D), row(ln2_g[l]), row(ln2_b[l]), alpha)
    return x2.reshape(B, S, D)
```

```python
import functools
import math

import jax
import jax.numpy as jnp
from jax import lax
from jax.experimental import pallas as pl
from jax.experimental.pallas import tpu as pltpu

F32 = jnp.float32
BF16 = jnp.bfloat16

N_HEADS = 32
N_KV_HEADS = 4
GROUP = N_HEADS // N_KV_HEADS
HEAD_DIM = 64
WINDOW = 128
ROPE_THETA = 500000.0
ROT_DIM = HEAD_DIM // 4
ATTN_SCALE = 1.0 / math.sqrt(HEAD_DIM)
PEER_HEADS = 8
N_KEYS = 128
TOPK = 16
LN_EPS = 1e-5

LANES = 128
SUBLANES = 8
VMEM_LIMIT = 56 * 1024 * 1024

TM_PROJ = 1024
TN_PROJ = 512
TS_MIX = 256
HALO = 32
TT_TOPK = 256
TT_PEER = 128
N_SPLIT = 2
PAIR_GROUP = 16
ITEM_UNROLL = 8
PACK_ROWS = 256


def _params(sem):
    return pltpu.CompilerParams(dimension_semantics=sem, vmem_limit_bytes=VMEM_LIMIT)


def _glu_kernel(x_ref, wa_ref, wb_ref, ba_ref, bb_ref, o_ref, xb_ref):
    @pl.when(pl.program_id(1) == 0)
    def _():
        xb_ref[...] = x_ref[...].astype(BF16)

    xb = xb_ref[...]
    za = jnp.dot(xb, wa_ref[...], preferred_element_type=F32) + ba_ref[...]
    zb = jnp.dot(xb, wb_ref[...], preferred_element_type=F32) + bb_ref[...]
    o_ref[...] = za * jax.nn.sigmoid(zb)


def _rope_block(z, c, s1, s2):
    half = ROT_DIM // 2
    return z * c + pltpu.roll(z, LANES - half, axis=1) * s1 + pltpu.roll(z, half, axis=1) * s2


def _qkv_kernel(n_rope_cols, x_ref, w_ref, b_ref, c_ref, s1_ref, s2_ref, o_ref, xb_ref):
    j = pl.program_id(1)

    @pl.when(j == 0)
    def _():
        xb_ref[...] = x_ref[...].astype(BF16)

    z = jnp.dot(xb_ref[...], w_ref[...], preferred_element_type=F32) + b_ref[...]
    c, s1, s2 = c_ref[...], s1_ref[...], s2_ref[...]
    tn = z.shape[1]
    for k in range(tn // LANES):
        sl = slice(LANES * k, LANES * (k + 1))
        rotary = j * tn + LANES * k < n_rope_cols
        o_ref[:, sl] = jnp.where(rotary, _rope_block(z[:, sl], c, s1, s2), z[:, sl]).astype(o_ref.dtype)


def _gates_kernel(x_ref, w_ref, b_ref, o_ref, xb_ref):
    @pl.when(pl.program_id(1) == 0)
    def _():
        xb_ref[...] = x_ref[...].astype(BF16)

    z = jnp.dot(xb_ref[...], w_ref[...], preferred_element_type=F32) + b_ref[...]
    o_ref[...] = jax.nn.sigmoid(z).astype(o_ref.dtype)


def _proj_specs(T, D, N, n_w, extra_specs=()):
    tm, tn = TM_PROJ, TN_PROJ
    assert T % tm == 0 and N % tn == 0, (T, N)
    grid = (T // tm, N // tn)
    in_specs = [pl.BlockSpec((tm, D), lambda i, j: (i, 0))]
    in_specs += [pl.BlockSpec((D, tn), lambda i, j: (0, j)) for _ in range(n_w)]
    in_specs += [pl.BlockSpec((1, tn), lambda i, j: (0, j)) for _ in range(n_w)]
    in_specs += list(extra_specs)
    out_spec = pl.BlockSpec((tm, tn), lambda i, j: (i, j))
    scratch = [pltpu.VMEM((tm, D), BF16)]
    return grid, in_specs, out_spec, scratch


def _glu(x2, wa, wb, ba, bb):
    T, D = x2.shape
    N = wa.shape[1]
    grid, in_specs, out_spec, scratch = _proj_specs(T, D, N, 2)
    return pl.pallas_call(
        _glu_kernel, out_shape=jax.ShapeDtypeStruct((T, N), F32), grid=grid,
        in_specs=in_specs, out_specs=out_spec, scratch_shapes=scratch,
        compiler_params=_params(("parallel", "arbitrary")), name="glu")(x2, wa, wb, ba, bb)


def _qkv(x2, w, b, c, s1, s2, n_rope_cols):
    T, D = x2.shape
    N = w.shape[1]
    tab = pl.BlockSpec((TM_PROJ, LANES), lambda i, j: (i, 0))
    grid, in_specs, out_spec, scratch = _proj_specs(T, D, N, 1, (tab, tab, tab))
    return pl.pallas_call(
        functools.partial(_qkv_kernel, n_rope_cols),
        out_shape=jax.ShapeDtypeStruct((T, N), BF16), grid=grid,
        in_specs=in_specs, out_specs=out_spec, scratch_shapes=scratch,
        compiler_params=_params(("parallel", "arbitrary")), name="qkv")(x2, w, b, c, s1, s2)


def _gates(x2, w, b):
    T, D = x2.shape
    N = w.shape[1]
    grid, in_specs, out_spec, scratch = _proj_specs(T, D, N, 1)
    return pl.pallas_call(
        _gates_kernel, out_shape=jax.ShapeDtypeStruct((T, N), BF16), grid=grid,
        in_specs=in_specs, out_specs=out_spec, scratch_shapes=scratch,
        compiler_params=_params(("parallel", "arbitrary")), name="gates")(x2, w, b)


def _attn_kernel(sink_ref, q_ref, kc_ref, kp_ref, vc_ref, vp_ref, o_ref):
    n = pl.program_id(1)
    blk = WINDOW
    lane = lax.broadcasted_iota(jnp.int32, (blk, LANES), 1)
    lo = lane < HEAD_DIM
    lane2 = lax.broadcasted_iota(jnp.int32, (2 * blk, LANES), 1)
    lo2 = lane2 < HEAD_DIM
    qi = lax.broadcasted_iota(jnp.int32, (blk, 2 * blk), 0)
    sj = lax.broadcasted_iota(jnp.int32, (blk, 2 * blk), 1)
    valid = (sj > qi) & (sj <= qi + blk) & (sj >= blk * (1 - n))
    neg = jnp.finfo(F32).min
    kk = jnp.concatenate([kp_ref[...], kc_ref[...]], axis=0).astype(F32)
    vv = jnp.concatenate([vp_ref[...], vc_ref[...]], axis=0).astype(F32)
    heads_per_tile = LANES // HEAD_DIM
    tiles_per_group = GROUP // heads_per_tile
    for g in range(N_KV_HEADS):
        tsl = slice(LANES * (g // heads_per_tile), LANES * (g // heads_per_tile + 1))

        def both_halves(t):
            r = pltpu.roll(t, HEAD_DIM, axis=1)
            return (jnp.where(lo2, t, r) if g % heads_per_tile == 0 else jnp.where(lo2, r, t)).astype(BF16)

        kd = both_halves(kk[:, tsl])
        vd = both_halves(vv[:, tsl])
        qs = []
        for c in range(tiles_per_group):
            col = LANES * (tiles_per_group * g + c)
            qt = q_ref[:, col:col + LANES]
            zero = jnp.zeros_like(qt)
            qs.append(jnp.where(lo, qt, zero))
            qs.append(jnp.where(lo, zero, qt))
        q8 = jnp.concatenate(qs, axis=0)
        s = lax.dot_general(q8, kd, (((1,), (1,)), ((), ())), preferred_element_type=F32) * ATTN_SCALE
        s3 = s.reshape(GROUP, blk, 2 * blk)
        ps = []
        half = GROUP // 2
        for h0 in range(0, GROUP, half):
            sh = jnp.where(valid[None], s3[h0:h0 + half], neg)
            sink = sink_ref[GROUP * g + h0:GROUP * g + h0 + half]
            m = jnp.maximum(jnp.max(sh, axis=-1, keepdims=True), sink)
            p = jnp.exp(sh - m)
            denom = jnp.sum(p, axis=-1, keepdims=True) + jnp.exp(sink - m)
            ps.append((p * (1.0 / denom)).astype(BF16))
        p = jnp.concatenate(ps, axis=0).reshape(GROUP * blk, 2 * blk)
        o = jnp.dot(p, vd, preferred_element_type=F32)
        for c in range(tiles_per_group):
            oa = o[blk * (2 * c):blk * (2 * c + 1)]
            ob = o[blk * (2 * c + 1):blk * (2 * c + 2)]
            col = LANES * (tiles_per_group * g + c)
            o_ref[:, col:col + LANES] = jnp.where(lo, oa, ob).astype(o_ref.dtype)


def _attention(qkv, sinks, B, S):
    T = B * S
    blk = WINDOW
    nb = S // blk
    qw = N_HEADS * HEAD_DIM
    kvw = N_KV_HEADS * HEAD_DIM
    k_col = qw // kvw
    v_col = k_col + 1
    row = lambda b, n: b * nb + n
    prev = lambda b, n: jnp.maximum(b * nb + n - 1, 0)
    in_specs = [
        pl.BlockSpec((N_HEADS, 1, 1), lambda b, n: (0, 0, 0)),
        pl.BlockSpec((blk, qw), lambda b, n: (row(b, n), 0)),
        pl.BlockSpec((blk, kvw), lambda b, n: (row(b, n), k_col)),
        pl.BlockSpec((blk, kvw), lambda b, n: (prev(b, n), k_col)),
        pl.BlockSpec((blk, kvw), lambda b, n: (row(b, n), v_col)),
        pl.BlockSpec((blk, kvw), lambda b, n: (prev(b, n), v_col)),
    ]
    return pl.pallas_call(
        _attn_kernel, out_shape=jax.ShapeDtypeStruct((T, qw), BF16), grid=(B, nb),
        in_specs=in_specs, out_specs=pl.BlockSpec((blk, qw), lambda b, n: (row(b, n), 0)),
        compiler_params=_params(("parallel", "arbitrary")), name="attn",
    )(sinks.astype(F32).reshape(N_HEADS, 1, 1), qkv, qkv, qkv, qkv, qkv)


def _layer_norm(x, g, b):
    mu = jnp.mean(x, axis=-1, keepdims=True)
    xc = x - mu
    var = jnp.mean(xc * xc, axis=-1, keepdims=True)
    return xc * lax.rsqrt(var + LN_EPS) * g + b


def _mix_kernel(alpha, u_ref, uh_ref, cw_ref, cb_ref, cg_ref, cbeta_ref, wco_ref, sa_ref, sb_ref,
                ya_ref, wmo_ref, x_ref, g1_ref, b1_ref, x1_ref, x1b_ref, win_ref, conv_ref, shift_ref):
    s = pl.program_id(1)
    ts, dc = u_ref.shape
    width = cw_ref.shape[0]
    win_ref[0:HALO, :] = jnp.where(s == 0, 0.0, uh_ref[...])
    win_ref[HALO:HALO + ts, :] = u_ref[...]
    first = HALO - (width - 1)
    span = ts + HALO - SUBLANES
    for c in range(dc // LANES):
        cs = slice(LANES * c, LANES * (c + 1))
        for r in range(1, SUBLANES):
            shift_ref[r, 0:span, :] = win_ref[r:r + span, cs]
        acc = jnp.broadcast_to(cb_ref[:, cs], (ts, LANES))
        for j in range(width):
            r, a = (first + j) % SUBLANES, (first + j) // SUBLANES * SUBLANES
            rows = win_ref[a:a + ts, cs] if r == 0 else shift_ref[r, a:a + ts, :]
            acc = acc + cw_ref[j:j + 1, cs] * rows
        conv_ref[:, cs] = acc
    h = _layer_norm(conv_ref[...], cg_ref[...], cbeta_ref[...])
    h = h * jax.nn.sigmoid(h)
    yc = jnp.dot(h.astype(BF16), wco_ref[...], preferred_element_type=F32)
    m = sa_ref[...].astype(F32) * yc + sb_ref[...].astype(F32) * ya_ref[...].astype(F32)
    mix = jnp.dot(m.astype(BF16), wmo_ref[...], preferred_element_type=F32)
    x1 = _layer_norm(alpha * x_ref[...] + mix, g1_ref[...], b1_ref[...])
    x1_ref[...] = x1
    x1b_ref[...] = x1.astype(BF16)


def _mix(u, cw, cb, cg, cbeta, wco, gv, ya, wmo, x2, g1, b1, alpha, B, S):
    T, D = x2.shape
    dc = u.shape[1]
    ts = TS_MIX
    ns = S // ts
    row = lambda b, s: b * ns + s
    halo = lambda b, s: jnp.maximum((b * S + s * ts) // HALO - 1, 0)
    const = lambda shape: pl.BlockSpec(shape, lambda b, s: (0, 0))
    in_specs = [
        pl.BlockSpec((ts, dc), lambda b, s: (row(b, s), 0)),
        pl.BlockSpec((HALO, dc), lambda b, s: (halo(b, s), 0)),
        const(cw.shape), const((1, dc)), const((1, dc)), const((1, dc)),
        const(wco.shape),
        pl.BlockSpec((ts, D), lambda b, s: (row(b, s), 0)),
        pl.BlockSpec((ts, D), lambda b, s: (row(b, s), 1)),
        pl.BlockSpec((ts, D), lambda b, s: (row(b, s), 0)),
        const(wmo.shape),
        pl.BlockSpec((ts, D), lambda b, s: (row(b, s), 0)),
        const((1, D)), const((1, D)),
    ]
    out_spec = pl.BlockSpec((ts, D), lambda b, s: (row(b, s), 0))
    return pl.pallas_call(
        functools.partial(_mix_kernel, alpha),
        out_shape=(jax.ShapeDtypeStruct((T, D), F32), jax.ShapeDtypeStruct((T, D), BF16)),
        grid=(B, ns), in_specs=in_specs, out_specs=(out_spec, out_spec),
        scratch_shapes=[pltpu.VMEM((HALO + ts, dc), F32), pltpu.VMEM((ts, dc), F32),
                        pltpu.VMEM((SUBLANES, HALO + ts, LANES), F32)],
        compiler_params=_params(("parallel", "arbitrary")), name="mix",
    )(u, u, cw, cb, cg, cbeta, wco, gv, gv, ya, wmo, x2, g1, b1)


def _topk_axis0(s, k, payload=None, rank=None):
    iota = lax.broadcasted_iota(jnp.int32, s.shape, 0) if rank is None else rank
    big = jnp.iinfo(jnp.int32).max
    vals, outs = [], []
    for _ in range(k):
        m = jnp.max(s, axis=0, keepdims=True)
        idx = jnp.min(jnp.where(s == m, iota, big), axis=0, keepdims=True)
        sel = iota == idx
        vals.append(m)
        if payload is None:
            outs.append(idx)
        else:
            outs.append(jnp.max(jnp.where(sel, payload, -1), axis=0, keepdims=True))
        s = jnp.where(sel, -jnp.inf, s)
    return jnp.concatenate(vals, axis=0), jnp.concatenate(outs, axis=0)


def _candidate_grid(v1, i1, v2, i2):
    k, t = v1.shape
    sub = lax.broadcasted_iota(jnp.int32, (SUBLANES, t), 0)
    vals, ids, flats = [], [], []
    for a in range(k // 2):
        nb = k // (a + 1)
        for b0 in range(0, nb, SUBLANES):
            live = sub < (nb - b0)
            vals.append(jnp.where(live, v1[a:a + 1] + v2[b0:b0 + SUBLANES], -jnp.inf))
            ids.append(i1[a:a + 1] * N_KEYS + i2[b0:b0 + SUBLANES])
            flats.append(a * k + b0 + sub)
    for a0 in range(k // 2, k, SUBLANES):
        vals.append(v1[a0:a0 + SUBLANES] + v2[0:1])
        ids.append(i1[a0:a0 + SUBLANES] * N_KEYS + i2[0:1])
        flats.append((a0 + sub) * k)
    return jnp.concatenate(vals, axis=0), jnp.concatenate(ids, axis=0), jnp.concatenate(flats, axis=0)


def _partner_rows(x, j):
    n = x.shape[0]
    if j >= SUBLANES:
        return jnp.concatenate([x[(b ^ 1) * j:((b ^ 1) + 1) * j] for b in range(n // j)], axis=0)
    sub = lax.broadcasted_iota(jnp.int32, (SUBLANES, x.shape[1]), 0)
    low = (sub & j) == 0
    out = []
    for r in range(n // SUBLANES):
        slab = x[SUBLANES * r:SUBLANES * (r + 1)]
        out.append(jnp.where(low, pltpu.roll(slab, SUBLANES - j, axis=0), pltpu.roll(slab, j, axis=0)))
    return jnp.concatenate(out, axis=0)


def _sort_axis0(key, val):
    n = key.shape[0]
    row = lax.broadcasted_iota(jnp.int32, key.shape, 0)
    k = 2
    while k <= n:
        j = k // 2
        while j >= 1:
            pk, pv = _partner_rows(key, j), _partner_rows(val, j)
            sign = 1 - 2 * (((row & j) != 0).astype(jnp.int32) ^ ((row & k) != 0).astype(jnp.int32))
            take = (pk - key) * sign < 0
            key = jnp.where(take, pk, key)
            val = jnp.where(take, pv, val)
            j //= 2
        k *= 2
    return key, val


def _topk_kernel(per_split, x_ref, wq_ref, k1_ref, k2_ref, e_ref, g_ref, bnd_ref, *idx_refs):
    q = jnp.dot(x_ref[...], wq_ref[...], preferred_element_type=F32).astype(BF16)
    nt = (((1,), (1,)), ((), ()))
    half = k1_ref.shape[1]
    es, gs = [], []
    for h in range(PEER_HEADS):
        q1 = q[:, 2 * half * h:2 * half * h + half]
        q2 = q[:, 2 * half * h + half:2 * half * (h + 1)]
        s1 = lax.dot_general(k1_ref[...], q1, nt, preferred_element_type=F32)
        s2 = lax.dot_general(k2_ref[...], q2, nt, preferred_element_type=F32)
        v1, i1 = _topk_axis0(s1, TOPK)
        v2, i2 = _topk_axis0(s2, TOPK)
        cand, cidx, flat = _candidate_grid(v1, i1, v2, i2)
        best, eidx = _topk_axis0(cand, TOPK, payload=cidx, rank=flat)
        w = jnp.exp(best - best[0:1])
        es.append(eidx)
        gs.append(w / jnp.sum(w, axis=0, keepdims=True))
    e, g = _sort_axis0(jnp.concatenate(es, axis=0), jnp.concatenate(gs, axis=0))
    e_ref[...] = e
    g_ref[...] = g
    shift = PAIR_GROUP.bit_length() - 1
    bounds = []
    for part, idx_ref in enumerate(idx_refs):
        idx_ref[...] = (jnp.clip(e - part * per_split, 0, per_split - 1) * SUBLANES).T
        below = jnp.sum((e < part * per_split).astype(jnp.int32), axis=0, keepdims=True)
        upto = jnp.sum((e < (part + 1) * per_split).astype(jnp.int32), axis=0, keepdims=True)
        bounds += [below >> shift, (upto + (PAIR_GROUP - 1)) >> shift]
    pad = bnd_ref.shape[0] - len(bounds)
    bnd_ref[...] = jnp.concatenate(bounds + [jnp.zeros_like(bounds[0])] * pad, axis=0)


def _topk(x1b, wq, k1, k2, per_split):
    T, D = x1b.shape
    tt = TT_TOPK
    npair = PEER_HEADS * TOPK
    const = lambda shape: pl.BlockSpec(shape, lambda i: (0, 0))
    pair_major = pl.BlockSpec((npair, tt), lambda i: (0, i))
    token_major = pl.BlockSpec((tt, npair), lambda i: (i, 0))
    return pl.pallas_call(
        functools.partial(_topk_kernel, per_split),
        out_shape=(jax.ShapeDtypeStruct((npair, T), jnp.int32), jax.ShapeDtypeStruct((npair, T), F32),
                   jax.ShapeDtypeStruct((SUBLANES, T), jnp.int32))
        + tuple(jax.ShapeDtypeStruct((T, npair), jnp.int32) for _ in range(N_SPLIT)),
        grid=(T // tt,),
        in_specs=[pl.BlockSpec((tt, D), lambda i: (i, 0)), const(wq.shape), const(k1.shape), const(k2.shape)],
        out_specs=(pair_major, pair_major, pl.BlockSpec((SUBLANES, tt), lambda i: (0, i)))
        + tuple(token_major for _ in range(N_SPLIT)),
        compiler_params=_params(("parallel",)), name="topk",
    )(x1b, wq, k1, k2)


def _build_items(part, bnd_ref, items_ref, tt, groups):
    def per_token(t, n):
        first, last = bnd_ref[SUBLANES * t + 2 * part], bnd_ref[SUBLANES * t + 2 * part + 1]
        for g in range(groups):
            items_ref[n + g] = t * groups + first + g
        return n + (last - first)

    n = lax.fori_loop(0, tt, per_token, 0)
    final = items_ref[jnp.maximum(n - 1, 0)]
    for q in range(ITEM_UNROLL - 1):
        items_ref[n + q] = final
    return lax.shift_right_logical(n + (ITEM_UNROLL - 1), ITEM_UNROLL.bit_length() - 1)


def _pack_table(table, pieces):
    n, d = table.shape
    eb = min(PACK_ROWS, n)
    assert d == 2 * SUBLANES * LANES and n % eb == 0, (n, d)
    words = pl.pallas_call(
        _pack_kernel, out_shape=jax.ShapeDtypeStruct((n, SUBLANES, LANES), jnp.uint32), grid=(n // eb,),
        in_specs=[pl.BlockSpec((eb, d), lambda i: (i, 0))],
        out_specs=pl.BlockSpec((eb, SUBLANES, LANES), lambda i: (i, 0, 0)),
        compiler_params=_params(("parallel",)), name="pack")(table)
    return words.reshape(pieces, (n // pieces) * SUBLANES, LANES)


def _pack_kernel(t_ref, o_ref):
    x = t_ref[...]
    half = x.shape[1] // 2
    bits = pltpu.bitcast(x.astype(BF16).astype(F32), jnp.uint32)
    words = (bits[:, :half] >> 16) | bits[:, half:]
    for s in range(SUBLANES):
        o_ref[:, s, :] = words[:, LANES * s:LANES * (s + 1)]


def _table_row(tab_ref, row8):
    words = tab_ref[pl.ds(pl.multiple_of(row8, SUBLANES), SUBLANES), :]
    lo = pltpu.bitcast(words << 16, F32)
    hi = pltpu.bitcast(words & jnp.uint32(0xFFFF0000), F32)
    return lo, hi


def _rows_to_sublanes(parts):
    sub = lax.broadcasted_iota(jnp.int32, (SUBLANES, LANES), 0)
    bits = SUBLANES.bit_length() - 1
    parts = [parts[int(format(i, f"0{bits}b")[::-1], 2)] for i in range(SUBLANES)]
    step = SUBLANES // 2
    while len(parts) > 1:
        low = (sub & step) == 0
        nxt = []
        for a, b in zip(parts[0::2], parts[1::2]):
            if 2 * step == SUBLANES:
                nxt.append(jnp.where(low, a, b) + pltpu.roll(jnp.where(low, b, a), step, axis=0))
            else:
                t = jnp.where(low, a, pltpu.roll(b, step, axis=0))
                u = jnp.where(low, pltpu.roll(a, SUBLANES - step, axis=0), b)
                nxt.append(t + u)
        parts = nxt
        step //= 2
    return parts[0]


def _hside_kernel(part, bnd_ref, idx_ref, x_ref, u_ref, h_ref, items_ref, sums_ref):
    npair, tt = h_ref.shape
    groups = npair // PAIR_GROUP
    gshift = groups.bit_length() - 1
    nblk = _build_items(part, bnd_ref, items_ref, tt, groups)

    @pl.when(pl.program_id(0) == 0)
    def _():
        sums_ref[...] = jnp.zeros(sums_ref.shape, F32)

    def block(i, carry):
        for q in range(ITEM_UNROLL):
            it = items_ref[i * ITEM_UNROLL + q]
            t = lax.shift_right_logical(it, gshift)
            base = it * PAIR_GROUP
            xf = x_ref[t].astype(F32)
            x_lo, x_hi = xf[0:SUBLANES], xf[SUBLANES:2 * SUBLANES]
            for sub in range(PAIR_GROUP // SUBLANES):
                parts = []
                for r in range(SUBLANES):
                    u_lo, u_hi = _table_row(u_ref, idx_ref[base + sub * SUBLANES + r])
                    parts.append(x_lo * u_lo + x_hi * u_hi)
                row0 = pl.multiple_of(base + sub * SUBLANES, SUBLANES)
                sums_ref[pl.ds(row0, SUBLANES), :] = _rows_to_sublanes(parts)
        return carry

    lax.fori_loop(0, nblk, block, 0)

    lane = lax.broadcasted_iota(jnp.int32, (npair, tt), 1)
    h_ref[...] = jnp.zeros(h_ref.shape, F32)

    def finish(t, carry):
        rows = sums_ref[pl.ds(pl.multiple_of(t * npair, npair), npair), :]
        h_ref[...] = jnp.where(lane == t, jnp.sum(rows, axis=1, keepdims=True), h_ref[...])
        return carry

    lax.fori_loop(0, tt, finish, 0, unroll=32)


def _peer_specs(npair, tt):
    bnd = pl.BlockSpec((tt * SUBLANES,), lambda i: (i,), memory_space=pltpu.SMEM)
    idx = pl.BlockSpec((tt * npair,), lambda i: (i,), memory_space=pltpu.SMEM)
    return bnd, idx


def _items_scratch(npair, tt):
    return pltpu.SMEM((tt * (npair // PAIR_GROUP) + PAIR_GROUP + ITEM_UNROLL,), jnp.int32)


def _hside(part, bnd, idx, x1p, table, npair):
    T = x1p.shape[0]
    tt = TT_PEER
    return pl.pallas_call(
        functools.partial(_hside_kernel, part),
        out_shape=jax.ShapeDtypeStruct((npair, T), F32), grid=(T // tt,),
        in_specs=[*_peer_specs(npair, tt),
                  pl.BlockSpec((tt,) + x1p.shape[1:], lambda i: (i, 0, 0)),
                  pl.BlockSpec(memory_space=pltpu.VMEM)],
        out_specs=pl.BlockSpec((npair, tt), lambda i: (0, i)),
        scratch_shapes=[_items_scratch(npair, tt), pltpu.VMEM((tt * npair, LANES), F32)],
        compiler_params=_params(("arbitrary",)), name="hside",
    )(bnd, idx, x1p, table)


def _act_kernel(per_split, e_ref, g_ref, *refs):
    h_refs, a_refs = refs[:N_SPLIT], refs[N_SPLIT:]
    e = e_ref[...]
    in_part = [(e >= k * per_split) & (e < (k + 1) * per_split) for k in range(N_SPLIT)]
    h = h_refs[0][...]
    for k in range(1, N_SPLIT):
        h = jnp.where(in_part[k], h_refs[k][...], h)
    a = 0.5 * h * (1.0 + lax.erf(h * math.sqrt(0.5))) * g_ref[...]
    for k in range(N_SPLIT):
        a_refs[k][...] = jnp.where(in_part[k], a, 0.0).T


def _act(e, g, hs, per_split):
    npair, T = e.shape
    tt = min(1024, T)
    pair_major = pl.BlockSpec((npair, tt), lambda i: (0, i))
    token_major = pl.BlockSpec((tt, npair), lambda i: (i, 0))
    return pl.pallas_call(
        functools.partial(_act_kernel, per_split),
        out_shape=tuple(jax.ShapeDtypeStruct((T, npair), F32) for _ in range(N_SPLIT)),
        grid=(T // tt,), in_specs=[pair_major] * (2 + N_SPLIT),
        out_specs=tuple([token_major] * N_SPLIT),
        compiler_params=_params(("parallel",)), name="act",
    )(e, g, *hs)


def _yside_kernel(part, has_prev, npair, bnd_ref, idx_ref, a_ref, v_ref, *refs):
    y_ref, items_ref, part_ref = refs[-3], refs[-2], refs[-1]
    tt = y_ref.shape[0]
    groups = npair // PAIR_GROUP
    n_acc = 4
    nblk = _build_items(part, bnd_ref, items_ref, tt, groups)

    @pl.when(pl.program_id(0) == 0)
    def _():
        part_ref[...] = jnp.zeros(part_ref.shape, F32)

    def block(i, carry):
        for q in range(ITEM_UNROLL):
            it = items_ref[i * ITEM_UNROLL + q]
            base = it * PAIR_GROUP
            los, his = [None] * n_acc, [None] * n_acc
            for r in range(PAIR_GROUP):
                w = a_ref[base + r]
                v_lo, v_hi = _table_row(v_ref, idx_ref[base + r])
                k = r % n_acc
                los[k] = w * v_lo if los[k] is None else los[k] + w * v_lo
                his[k] = w * v_hi if his[k] is None else his[k] + w * v_hi
            lo = (los[0] + los[1]) + (los[2] + los[3])
            hi = (his[0] + his[1]) + (his[2] + his[3])
            part_ref[it] = jnp.concatenate([lo, hi], axis=0)
        return carry

    lax.fori_loop(0, nblk, block, 0)

    group_id = lax.broadcasted_iota(jnp.int32, (groups,) + part_ref.shape[1:], 0)

    def finish(t, carry):
        first, last = bnd_ref[SUBLANES * t + 2 * part], bnd_ref[SUBLANES * t + 2 * part + 1]
        sums = part_ref[pl.ds(pl.multiple_of(t * groups, groups), groups)]
        live = (group_id - first).astype(jnp.uint32) < (last - first).astype(jnp.uint32)
        y = jnp.sum(jnp.where(live, sums, 0.0), axis=0)
        if has_prev:
            y = y + refs[0][t]
        y_ref[t] = y
        return carry

    lax.fori_loop(0, tt, finish, 0, unroll=8)


def _yside(part, bnd, idx, a, table, prev, npair, packed):
    T = idx.shape[0] // npair
    tt = TT_PEER
    tile = pl.BlockSpec((tt,) + packed, lambda i: (i, 0, 0))
    bnd_spec, idx_spec = _peer_specs(npair, tt)
    in_specs = [bnd_spec, idx_spec, idx_spec, pl.BlockSpec(memory_space=pltpu.VMEM)]
    args = [bnd, idx, a, table]
    if prev is not None:
        in_specs.append(tile)
        args.append(prev)
    return pl.pallas_call(
        functools.partial(_yside_kernel, part, prev is not None, npair),
        out_shape=jax.ShapeDtypeStruct((T,) + packed, F32), grid=(T // tt,),
        in_specs=in_specs, out_specs=tile,
        scratch_shapes=[_items_scratch(npair, tt),
                        pltpu.VMEM((tt * (npair // PAIR_GROUP),) + packed, F32)],
        compiler_params=_params(("arbitrary",)), name="yside",
    )(*args)


def _ln2_kernel(alpha, x_ref, y_ref, g_ref, b_ref, o_ref):
    o_ref[...] = _layer_norm(alpha * x_ref[...] + y_ref[...], g_ref[...], b_ref[...])


def _ln2(x1, y, g, b, alpha):
    T, D = x1.shape
    tm = 512
    tile = pl.BlockSpec((tm, D), lambda i: (i, 0))
    vec = pl.BlockSpec((1, D), lambda i: (0, 0))
    return pl.pallas_call(
        functools.partial(_ln2_kernel, alpha), out_shape=jax.ShapeDtypeStruct((T, D), F32),
        grid=(T // tm,), in_specs=[tile, tile, vec, vec], out_specs=tile,
        compiler_params=_params(("parallel",)), name="ln2")(x1, y, g, b)


def _rope_tables(positions):
    T = positions.size
    half = ROT_DIM // 2
    inv_freq = ROPE_THETA ** (-jnp.arange(0, ROT_DIM, 2, dtype=F32) / ROT_DIM)
    ang = positions.reshape(T, 1).astype(F32) * inv_freq
    cos, sin = jnp.cos(ang), jnp.sin(ang)
    rest = HEAD_DIM - ROT_DIM
    zeros_h = jnp.zeros((T, half), F32)
    c = jnp.concatenate([cos, cos, jnp.ones((T, rest), F32)], axis=-1)
    s1 = jnp.concatenate([-sin, zeros_h, jnp.zeros((T, rest), F32)], axis=-1)
    s2 = jnp.concatenate([zeros_h, sin, jnp.zeros((T, rest), F32)], axis=-1)
    rep = LANES // HEAD_DIM
    return tuple(jnp.tile(t, (1, rep)) for t in (c, s1, s2))


def kernel(x, positions, w_in, b_in, conv_dw_w, conv_dw_b, conv_ln_g, conv_ln_b, w_conv_out, attn_sinks, w_mix_out, ln1_g, ln1_b, w_peer_q, sub_keys_1, sub_keys_2, expert_u, expert_v, ln2_g, ln2_b):
    B, S, D = x.shape
    T = B * S
    depth = w_in.shape[0]
    alpha = (2 * depth) ** 0.25
    dc = conv_dw_w.shape[-1]
    qw = N_HEADS * HEAD_DIM
    kvw = N_KV_HEADS * HEAD_DIM
    n_exp = expert_u.shape[1]
    per_split = n_exp // N_SPLIT
    packed = (D // LANES, LANES)
    c, s1, s2 = _rope_tables(positions)
    row = lambda v: v.reshape(1, -1)

    x2 = x.reshape(T, D)
    for l in range(depth):
        w = w_in[l].astype(BF16)
        b = b_in[l]
        o_q = 2 * dc
        o_k, o_v, o_ga = o_q + qw, o_q + qw + kvw, o_q + qw + 2 * kvw
        u = _glu(x2, w[:, :dc], w[:, dc:2 * dc], row(b[:dc]), row(b[dc:2 * dc]))
        qkv = _qkv(x2, w[:, o_q:o_ga], row(b[o_q:o_ga]), c, s1, s2, qw + kvw)
        gv = _gates(x2, w[:, o_ga:], row(b[o_ga:]))
        ya = _attention(qkv, attn_sinks[l], B, S)
        x1, x1b = _mix(u, conv_dw_w[l].reshape(-1, dc), row(conv_dw_b[l]), row(conv_ln_g[l]),
                       row(conv_ln_b[l]), w_conv_out[l].astype(BF16), gv, ya,
                       w_mix_out[l].astype(BF16), x2, row(ln1_g[l]), row(ln1_b[l]), alpha, B, S)
        outs = _topk(x1b, w_peer_q[l].astype(BF16), sub_keys_1[l].astype(BF16),
                     sub_keys_2[l].astype(BF16), per_split)
        e, g = outs[0], outs[1]
        npair = e.shape[0]
        bnd = outs[2].T.reshape(-1)
        idxs = [i.reshape(-1) for i in outs[3:]]
        x1p = x1b.reshape((T,) + packed)
        ut = _pack_table(expert_u[l], N_SPLIT)
        vt = _pack_table(expert_v[l], N_SPLIT)
        hs = [_hside(k, bnd, idxs[k], x1p, ut[k], npair) for k in range(N_SPLIT)]
        acts = _act(e, g, hs, per_split)
        y = None
        for k in range(N_SPLIT):
            y = _yside(k, bnd, idxs[k], acts[k].reshape(-1), vt[k], y, npair, packed)
        x2 = _ln2(x1, y.reshape(T, D), row(ln2_g[l]), row(ln2_b[l]), alpha)
    return x2.reshape(B, S, D)
```

```python
import functools
import math

import jax
import jax.numpy as jnp
from jax import lax
from jax.experimental import pallas as pl
from jax.experimental.pallas import tpu as pltpu

F32 = jnp.float32
BF16 = jnp.bfloat16

N_HEADS = 32
N_KV_HEADS = 4
GROUP = N_HEADS // N_KV_HEADS
HEAD_DIM = 64
WINDOW = 128
ROPE_THETA = 500000.0
ROT_DIM = HEAD_DIM // 4
ATTN_SCALE = 1.0 / math.sqrt(HEAD_DIM)
PEER_HEADS = 8
N_KEYS = 128
TOPK = 16
LN_EPS = 1e-5

LANES = 128
SUBLANES = 8
VMEM_LIMIT = 56 * 1024 * 1024

TM_PROJ = 1024
TN_PROJ = 512
TS_MIX = 256
HALO = 32
TT_TOPK = 256
TT_PEER = 128
N_SPLIT = 2
PAIR_GROUP = 16
ITEM_UNROLL = 8
PACK_ROWS = 256


def _params(sem):
    return pltpu.CompilerParams(dimension_semantics=sem, vmem_limit_bytes=VMEM_LIMIT)


def _glu_kernel(x_ref, wa_ref, wb_ref, ba_ref, bb_ref, o_ref, xb_ref):
    @pl.when(pl.program_id(1) == 0)
    def _():
        xb_ref[...] = x_ref[...].astype(BF16)

    xb = xb_ref[...]
    za = jnp.dot(xb, wa_ref[...], preferred_element_type=F32) + ba_ref[...]
    zb = jnp.dot(xb, wb_ref[...], preferred_element_type=F32) + bb_ref[...]
    o_ref[...] = za * jax.nn.sigmoid(zb)


def _rope_block(z, c, s1, s2):
    half = ROT_DIM // 2
    return z * c + pltpu.roll(z, LANES - half, axis=1) * s1 + pltpu.roll(z, half, axis=1) * s2


def _qkv_kernel(n_rope_cols, x_ref, w_ref, b_ref, c_ref, s1_ref, s2_ref, o_ref, xb_ref):
    j = pl.program_id(1)

    @pl.when(j == 0)
    def _():
        xb_ref[...] = x_ref[...].astype(BF16)

    z = jnp.dot(xb_ref[...], w_ref[...], preferred_element_type=F32) + b_ref[...]
    c, s1, s2 = c_ref[...], s1_ref[...], s2_ref[...]
    tn = z.shape[1]
    for k in range(tn // LANES):
        sl = slice(LANES * k, LANES * (k + 1))
        rotary = j * tn + LANES * k < n_rope_cols
        o_ref[:, sl] = jnp.where(rotary, _rope_block(z[:, sl], c, s1, s2), z[:, sl]).astype(o_ref.dtype)


def _gates_kernel(x_ref, w_ref, b_ref, o_ref, xb_ref):
    @pl.when(pl.program_id(1) == 0)
    def _():
        xb_ref[...] = x_ref[...].astype(BF16)

    z = jnp.dot(xb_ref[...], w_ref[...], preferred_element_type=F32) + b_ref[...]
    o_ref[...] = jax.nn.sigmoid(z).astype(o_ref.dtype)


def _proj_specs(T, D, N, n_w, extra_specs=()):
    tm, tn = TM_PROJ, TN_PROJ
    assert T % tm == 0 and N % tn == 0, (T, N)
    grid = (T // tm, N // tn)
    in_specs = [pl.BlockSpec((tm, D), lambda i, j: (i, 0))]
    in_specs += [pl.BlockSpec((D, tn), lambda i, j: (0, j)) for _ in range(n_w)]
    in_specs += [pl.BlockSpec((1, tn), lambda i, j: (0, j)) for _ in range(n_w)]
    in_specs += list(extra_specs)
    out_spec = pl.BlockSpec((tm, tn), lambda i, j: (i, j))
    scratch = [pltpu.VMEM((tm, D), BF16)]
    return grid, in_specs, out_spec, scratch


def _glu(x2, wa, wb, ba, bb):
    T, D = x2.shape
    N = wa.shape[1]
    grid, in_specs, out_spec, scratch = _proj_specs(T, D, N, 2)
    return pl.pallas_call(
        _glu_kernel, out_shape=jax.ShapeDtypeStruct((T, N), F32), grid=grid,
        in_specs=in_specs, out_specs=out_spec, scratch_shapes=scratch,
        compiler_params=_params(("parallel", "arbitrary")), name="glu")(x2, wa, wb, ba, bb)


def _qkv(x2, w, b, c, s1, s2, n_rope_cols):
    T, D = x2.shape
    N = w.shape[1]
    tab = pl.BlockSpec((TM_PROJ, LANES), lambda i, j: (i, 0))
    grid, in_specs, out_spec, scratch = _proj_specs(T, D, N, 1, (tab, tab, tab))
    return pl.pallas_call(
        functools.partial(_qkv_kernel, n_rope_cols),
        out_shape=jax.ShapeDtypeStruct((T, N), BF16), grid=grid,
        in_specs=in_specs, out_specs=out_spec, scratch_shapes=scratch,
        compiler_params=_params(("parallel", "arbitrary")), name="qkv")(x2, w, b, c, s1, s2)


def _gates(x2, w, b):
    T, D = x2.shape
    N = w.shape[1]
    grid, in_specs, out_spec, scratch = _proj_specs(T, D, N, 1)
    return pl.pallas_call(
        _gates_kernel, out_shape=jax.ShapeDtypeStruct((T, N), BF16), grid=grid,
        in_specs=in_specs, out_specs=out_spec, scratch_shapes=scratch,
        compiler_params=_params(("parallel", "arbitrary")), name="gates")(x2, w, b)


def _attn_kernel(sink_ref, q_ref, kc_ref, kp_ref, vc_ref, vp_ref, o_ref):
    n = pl.program_id(1)
    blk = WINDOW
    lane = lax.broadcasted_iota(jnp.int32, (blk, LANES), 1)
    lo = lane < HEAD_DIM
    lane2 = lax.broadcasted_iota(jnp.int32, (2 * blk, LANES), 1)
    lo2 = lane2 < HEAD_DIM
    qi = lax.broadcasted_iota(jnp.int32, (blk, 2 * blk), 0)
    sj = lax.broadcasted_iota(jnp.int32, (blk, 2 * blk), 1)
    valid = (sj > qi) & (sj <= qi + blk) & (sj >= blk * (1 - n))
    neg = jnp.finfo(F32).min
    kk = jnp.concatenate([kp_ref[...], kc_ref[...]], axis=0).astype(F32)
    vv = jnp.concatenate([vp_ref[...], vc_ref[...]], axis=0).astype(F32)
    heads_per_tile = LANES // HEAD_DIM
    tiles_per_group = GROUP // heads_per_tile
    for g in range(N_KV_HEADS):
        tsl = slice(LANES * (g // heads_per_tile), LANES * (g // heads_per_tile + 1))

        def both_halves(t):
            r = pltpu.roll(t, HEAD_DIM, axis=1)
            return (jnp.where(lo2, t, r) if g % heads_per_tile == 0 else jnp.where(lo2, r, t)).astype(BF16)

        kd = both_halves(kk[:, tsl])
        vd = both_halves(vv[:, tsl])
        qs = []
        for c in range(tiles_per_group):
            col = LANES * (tiles_per_group * g + c)
            qt = q_ref[:, col:col + LANES]
            zero = jnp.zeros_like(qt)
            qs.append(jnp.where(lo, qt, zero))
            qs.append(jnp.where(lo, zero, qt))
        q8 = jnp.concatenate(qs, axis=0)
        s = lax.dot_general(q8, kd, (((1,), (1,)), ((), ())), preferred_element_type=F32) * ATTN_SCALE
        s3 = s.reshape(GROUP, blk, 2 * blk)
        ps = []
        half = GROUP // 2
        for h0 in range(0, GROUP, half):
            sh = jnp.where(valid[None], s3[h0:h0 + half], neg)
            sink = sink_ref[GROUP * g + h0:GROUP * g + h0 + half]
            m = jnp.maximum(jnp.max(sh, axis=-1, keepdims=True), sink)
            p = jnp.exp(sh - m)
            denom = jnp.sum(p, axis=-1, keepdims=True) + jnp.exp(sink - m)
            ps.append((p * (1.0 / denom)).astype(BF16))
        p = jnp.concatenate(ps, axis=0).reshape(GROUP * blk, 2 * blk)
        o = jnp.dot(p, vd, preferred_element_type=F32)
        for c in range(tiles_per_group):
            oa = o[blk * (2 * c):blk * (2 * c + 1)]
            ob = o[blk * (2 * c + 1):blk * (2 * c + 2)]
            col = LANES * (tiles_per_group * g + c)
            o_ref[:, col:col + LANES] = jnp.where(lo, oa, ob).astype(o_ref.dtype)


def _attention(qkv, sinks, B, S):
    T = B * S
    blk = WINDOW
    nb = S // blk
    qw = N_HEADS * HEAD_DIM
    kvw = N_KV_HEADS * HEAD_DIM
    k_col = qw // kvw
    v_col = k_col + 1
    row = lambda b, n: b * nb + n
    prev = lambda b, n: jnp.maximum(b * nb + n - 1, 0)
    in_specs = [
        pl.BlockSpec((N_HEADS, 1, 1), lambda b, n: (0, 0, 0)),
        pl.BlockSpec((blk, qw), lambda b, n: (row(b, n), 0)),
        pl.BlockSpec((blk, kvw), lambda b, n: (row(b, n), k_col)),
        pl.BlockSpec((blk, kvw), lambda b, n: (prev(b, n), k_col)),
        pl.BlockSpec((blk, kvw), lambda b, n: (row(b, n), v_col)),
        pl.BlockSpec((blk, kvw), lambda b, n: (prev(b, n), v_col)),
    ]
    return pl.pallas_call(
        _attn_kernel, out_shape=jax.ShapeDtypeStruct((T, qw), BF16), grid=(B, nb),
        in_specs=in_specs, out_specs=pl.BlockSpec((blk, qw), lambda b, n: (row(b, n), 0)),
        compiler_params=_params(("parallel", "arbitrary")), name="attn",
    )(sinks.astype(F32).reshape(N_HEADS, 1, 1), qkv, qkv, qkv, qkv, qkv)


def _layer_norm(x, g, b):
    mu = jnp.mean(x, axis=-1, keepdims=True)
    xc = x - mu
    var = jnp.mean(xc * xc, axis=-1, keepdims=True)
    return xc * lax.rsqrt(var + LN_EPS) * g + b


def _mix_kernel(alpha, u_ref, uh_ref, cw_ref, cb_ref, cg_ref, cbeta_ref, wco_ref, sa_ref, sb_ref,
                ya_ref, wmo_ref, x_ref, g1_ref, b1_ref, x1_ref, x1b_ref, win_ref, conv_ref, shift_ref):
    s = pl.program_id(1)
    ts, dc = u_ref.shape
    width = cw_ref.shape[0]
    win_ref[0:HALO, :] = jnp.where(s == 0, 0.0, uh_ref[...])
    win_ref[HALO:HALO + ts, :] = u_ref[...]
    first = HALO - (width - 1)
    span = ts + HALO - SUBLANES
    for c in range(dc // LANES):
        cs = slice(LANES * c, LANES * (c + 1))
        for r in range(1, SUBLANES):
            shift_ref[r, 0:span, :] = win_ref[r:r + span, cs]
        acc = jnp.broadcast_to(cb_ref[:, cs], (ts, LANES))
        for j in range(width):
            r, a = (first + j) % SUBLANES, (first + j) // SUBLANES * SUBLANES
            rows = win_ref[a:a + ts, cs] if r == 0 else shift_ref[r, a:a + ts, :]
            acc = acc + cw_ref[j:j + 1, cs] * rows
        conv_ref[:, cs] = acc
    h = _layer_norm(conv_ref[...], cg_ref[...], cbeta_ref[...])
    h = h * jax.nn.sigmoid(h)
    yc = jnp.dot(h.astype(BF16), wco_ref[...], preferred_element_type=F32)
    m = sa_ref[...].astype(F32) * yc + sb_ref[...].astype(F32) * ya_ref[...].astype(F32)
    mix = jnp.dot(m.astype(BF16), wmo_ref[...], preferred_element_type=F32)
    x1 = _layer_norm(alpha * x_ref[...] + mix, g1_ref[...], b1_ref[...])
    x1_ref[...] = x1
    x1b_ref[...] = x1.astype(BF16)


def _mix(u, cw, cb, cg, cbeta, wco, gv, ya, wmo, x2, g1, b1, alpha, B, S):
    T, D = x2.shape
    dc = u.shape[1]
    ts = TS_MIX
    ns = S // ts
    row = lambda b, s: b * ns + s
    halo = lambda b, s: jnp.maximum((b * S + s * ts) // HALO - 1, 0)
    const = lambda shape: pl.BlockSpec(shape, lambda b, s: (0, 0))
    in_specs = [
        pl.BlockSpec((ts, dc), lambda b, s: (row(b, s), 0)),
        pl.BlockSpec((HALO, dc), lambda b, s: (halo(b, s), 0)),
        const(cw.shape), const((1, dc)), const((1, dc)), const((1, dc)),
        const(wco.shape),
        pl.BlockSpec((ts, D), lambda b, s: (row(b, s), 0)),
        pl.BlockSpec((ts, D), lambda b, s: (row(b, s), 1)),
        pl.BlockSpec((ts, D), lambda b, s: (row(b, s), 0)),
        const(wmo.shape),
        pl.BlockSpec((ts, D), lambda b, s: (row(b, s), 0)),
        const((1, D)), const((1, D)),
    ]
    out_spec = pl.BlockSpec((ts, D), lambda b, s: (row(b, s), 0))
    return pl.pallas_call(
        functools.partial(_mix_kernel, alpha),
        out_shape=(jax.ShapeDtypeStruct((T, D), F32), jax.ShapeDtypeStruct((T, D), BF16)),
        grid=(B, ns), in_specs=in_specs, out_specs=(out_spec, out_spec),
        scratch_shapes=[pltpu.VMEM((HALO + ts, dc), F32), pltpu.VMEM((ts, dc), F32),
                        pltpu.VMEM((SUBLANES, HALO + ts, LANES), F32)],
        compiler_params=_params(("parallel", "arbitrary")), name="mix",
    )(u, u, cw, cb, cg, cbeta, wco, gv, gv, ya, wmo, x2, g1, b1)


def _topk_axis0(s, k, payload=None, rank=None):
    iota = lax.broadcasted_iota(jnp.int32, s.shape, 0) if rank is None else rank
    big = jnp.iinfo(jnp.int32).max
    vals, outs = [], []
    for _ in range(k):
        m = jnp.max(s, axis=0, keepdims=True)
        idx = jnp.min(jnp.where(s == m, iota, big), axis=0, keepdims=True)
        sel = iota == idx
        vals.append(m)
        if payload is None:
            outs.append(idx)
        else:
            outs.append(jnp.max(jnp.where(sel, payload, -1), axis=0, keepdims=True))
        s = jnp.where(sel, -jnp.inf, s)
    return jnp.concatenate(vals, axis=0), jnp.concatenate(outs, axis=0)


def _candidate_grid(v1, i1, v2, i2):
    k, t = v1.shape
    sub = lax.broadcasted_iota(jnp.int32, (SUBLANES, t), 0)
    vals, ids, flats = [], [], []
    for a in range(k // 2):
        nb = k // (a + 1)
        for b0 in range(0, nb, SUBLANES):
            live = sub < (nb - b0)
            vals.append(jnp.where(live, v1[a:a + 1] + v2[b0:b0 + SUBLANES], -jnp.inf))
            ids.append(i1[a:a + 1] * N_KEYS + i2[b0:b0 + SUBLANES])
            flats.append(a * k + b0 + sub)
    for a0 in range(k // 2, k, SUBLANES):
        vals.append(v1[a0:a0 + SUBLANES] + v2[0:1])
        ids.append(i1[a0:a0 + SUBLANES] * N_KEYS + i2[0:1])
        flats.append((a0 + sub) * k)
    return jnp.concatenate(vals, axis=0), jnp.concatenate(ids, axis=0), jnp.concatenate(flats, axis=0)


def _partner_rows(x, j):
    n = x.shape[0]
    if j >= SUBLANES:
        return jnp.concatenate([x[(b ^ 1) * j:((b ^ 1) + 1) * j] for b in range(n // j)], axis=0)
    sub = lax.broadcasted_iota(jnp.int32, (SUBLANES, x.shape[1]), 0)
    low = (sub & j) == 0
    out = []
    for r in range(n // SUBLANES):
        slab = x[SUBLANES * r:SUBLANES * (r + 1)]
        out.append(jnp.where(low, pltpu.roll(slab, SUBLANES - j, axis=0), pltpu.roll(slab, j, axis=0)))
    return jnp.concatenate(out, axis=0)


def _sort_axis0(key, val):
    n = key.shape[0]
    row = lax.broadcasted_iota(jnp.int32, key.shape, 0)
    k = 2
    while k <= n:
        j = k // 2
        while j >= 1:
            pk, pv = _partner_rows(key, j), _partner_rows(val, j)
            sign = 1 - 2 * (((row & j) != 0).astype(jnp.int32) ^ ((row & k) != 0).astype(jnp.int32))
            take = (pk - key) * sign < 0
            key = jnp.where(take, pk, key)
            val = jnp.where(take, pv, val)
            j //= 2
        k *= 2
    return key, val


def _topk_kernel(per_split, x_ref, wq_ref, k1_ref, k2_ref, e_ref, g_ref, bnd_ref, *idx_refs):
    q = jnp.dot(x_ref[...], wq_ref[...], preferred_element_type=F32).astype(BF16)
    nt = (((1,), (1,)), ((), ()))
    half = k1_ref.shape[1]
    es, gs = [], []
    for h in range(PEER_HEADS):
        q1 = q[:, 2 * half * h:2 * half * h + half]
        q2 = q[:, 2 * half * h + half:2 * half * (h + 1)]
        s1 = lax.dot_general(k1_ref[...], q1, nt, preferred_element_type=F32)
        s2 = lax.dot_general(k2_ref[...], q2, nt, preferred_element_type=F32)
        v1, i1 = _topk_axis0(s1, TOPK)
        v2, i2 = _topk_axis0(s2, TOPK)
        cand, cidx, flat = _candidate_grid(v1, i1, v2, i2)
        best, eidx = _topk_axis0(cand, TOPK, payload=cidx, rank=flat)
        w = jnp.exp(best - best[0:1])
        es.append(eidx)
        gs.append(w / jnp.sum(w, axis=0, keepdims=True))
    e, g = _sort_axis0(jnp.concatenate(es, axis=0), jnp.concatenate(gs, axis=0))
    e_ref[...] = e
    g_ref[...] = g
    shift = PAIR_GROUP.bit_length() - 1
    bounds = []
    for part, idx_ref in enumerate(idx_refs):
        idx_ref[...] = (jnp.clip(e - part * per_split, 0, per_split - 1) * SUBLANES).T
        below = jnp.sum((e < part * per_split).astype(jnp.int32), axis=0, keepdims=True)
        upto = jnp.sum((e < (part + 1) * per_split).astype(jnp.int32), axis=0, keepdims=True)
        bounds += [below >> shift, (upto + (PAIR_GROUP - 1)) >> shift]
    pad = bnd_ref.shape[0] - len(bounds)
    bnd_ref[...] = jnp.concatenate(bounds + [jnp.zeros_like(bounds[0])] * pad, axis=0)


def _topk(x1b, wq, k1, k2, per_split):
    T, D = x1b.shape
    tt = TT_TOPK
    npair = PEER_HEADS * TOPK
    const = lambda shape: pl.BlockSpec(shape, lambda i: (0, 0))
    pair_major = pl.BlockSpec((npair, tt), lambda i: (0, i))
    token_major = pl.BlockSpec((tt, npair), lambda i: (i, 0))
    return pl.pallas_call(
        functools.partial(_topk_kernel, per_split),
        out_shape=(jax.ShapeDtypeStruct((npair, T), jnp.int32), jax.ShapeDtypeStruct((npair, T), F32),
                   jax.ShapeDtypeStruct((SUBLANES, T), jnp.int32))
        + tuple(jax.ShapeDtypeStruct((T, npair), jnp.int32) for _ in range(N_SPLIT)),
        grid=(T // tt,),
        in_specs=[pl.BlockSpec((tt, D), lambda i: (i, 0)), const(wq.shape), const(k1.shape), const(k2.shape)],
        out_specs=(pair_major, pair_major, pl.BlockSpec((SUBLANES, tt), lambda i: (0, i)))
        + tuple(token_major for _ in range(N_SPLIT)),
        compiler_params=_params(("parallel",)), name="topk",
    )(x1b, wq, k1, k2)


def _build_items(part, bnd_ref, items_ref, tt, groups):
    def per_token(t, n):
        first, last = bnd_ref[SUBLANES * t + 2 * part], bnd_ref[SUBLANES * t + 2 * part + 1]
        for g in range(groups):
            items_ref[n + g] = t * groups + first + g
        return n + (last - first)

    n = lax.fori_loop(0, tt, per_token, 0)
    final = items_ref[jnp.maximum(n - 1, 0)]
    for q in range(ITEM_UNROLL - 1):
        items_ref[n + q] = final
    return lax.shift_right_logical(n + (ITEM_UNROLL - 1), ITEM_UNROLL.bit_length() - 1)


def _pack_table(table, pieces):
    n, d = table.shape
    eb = min(PACK_ROWS, n)
    assert d == 2 * SUBLANES * LANES and n % eb == 0, (n, d)
    words = pl.pallas_call(
        _pack_kernel, out_shape=jax.ShapeDtypeStruct((n, SUBLANES, LANES), jnp.uint32), grid=(n // eb,),
        in_specs=[pl.BlockSpec((eb, d), lambda i: (i, 0))],
        out_specs=pl.BlockSpec((eb, SUBLANES, LANES), lambda i: (i, 0, 0)),
        compiler_params=_params(("parallel",)), name="pack")(table)
    return words.reshape(pieces, (n // pieces) * SUBLANES, LANES)


def _pack_kernel(t_ref, o_ref):
    x = t_ref[...]
    half = x.shape[1] // 2
    bits = pltpu.bitcast(x.astype(BF16).astype(F32), jnp.uint32)
    words = (bits[:, :half] >> 16) | bits[:, half:]
    for s in range(SUBLANES):
        o_ref[:, s, :] = words[:, LANES * s:LANES * (s + 1)]


def _table_row(tab_ref, row8):
    words = tab_ref[pl.ds(pl.multiple_of(row8, SUBLANES), SUBLANES), :]
    lo = pltpu.bitcast(words << 16, F32)
    hi = pltpu.bitcast(words & jnp.uint32(0xFFFF0000), F32)
    return lo, hi


def _rows_to_sublanes(parts):
    sub = lax.broadcasted_iota(jnp.int32, (SUBLANES, LANES), 0)
    bits = SUBLANES.bit_length() - 1
    parts = [parts[int(format(i, f"0{bits}b")[::-1], 2)] for i in range(SUBLANES)]
    step = SUBLANES // 2
    while len(parts) > 1:
        low = (sub & step) == 0
        nxt = []
        for a, b in zip(parts[0::2], parts[1::2]):
            if 2 * step == SUBLANES:
                nxt.append(jnp.where(low, a, b) + pltpu.roll(jnp.where(low, b, a), step, axis=0))
            else:
                t = jnp.where(low, a, pltpu.roll(b, step, axis=0))
                u = jnp.where(low, pltpu.roll(a, SUBLANES - step, axis=0), b)
                nxt.append(t + u)
        parts = nxt
        step //= 2
    return parts[0]


def _hside_kernel(part, bnd_ref, idx_ref, x_ref, u_ref, h_ref, items_ref, sums_ref):
    npair, tt = h_ref.shape
    groups = npair // PAIR_GROUP
    gshift = groups.bit_length() - 1
    nblk = _build_items(part, bnd_ref, items_ref, tt, groups)

    @pl.when(pl.program_id(0) == 0)
    def _():
        sums_ref[...] = jnp.zeros(sums_ref.shape, F32)

    def block(i, carry):
        for q in range(ITEM_UNROLL):
            it = items_ref[i * ITEM_UNROLL + q]
            t = lax.shift_right_logical(it, gshift)
            base = it * PAIR_GROUP
            xf = x_ref[t].astype(F32)
            x_lo, x_hi = xf[0:SUBLANES], xf[SUBLANES:2 * SUBLANES]
            for sub in range(PAIR_GROUP // SUBLANES):
                parts = []
                for r in range(SUBLANES):
                    u_lo, u_hi = _table_row(u_ref, idx_ref[base + sub * SUBLANES + r])
                    parts.append(x_lo * u_lo + x_hi * u_hi)
                row0 = pl.multiple_of(base + sub * SUBLANES, SUBLANES)
                sums_ref[pl.ds(row0, SUBLANES), :] = _rows_to_sublanes(parts)
        return carry

    lax.fori_loop(0, nblk, block, 0)

    lane = lax.broadcasted_iota(jnp.int32, (npair, tt), 1)
    h_ref[...] = jnp.zeros(h_ref.shape, F32)

    def finish(t, carry):
        rows = sums_ref[pl.ds(pl.multiple_of(t * npair, npair), npair), :]
        h_ref[...] = jnp.where(lane == t, jnp.sum(rows, axis=1, keepdims=True), h_ref[...])
        return carry

    lax.fori_loop(0, tt, finish, 0, unroll=32)


def _peer_specs(npair, tt):
    bnd = pl.BlockSpec((tt * SUBLANES,), lambda i: (i,), memory_space=pltpu.SMEM)
    idx = pl.BlockSpec((tt * npair,), lambda i: (i,), memory_space=pltpu.SMEM)
    return bnd, idx


def _items_scratch(npair, tt):
    return pltpu.SMEM((tt * (npair // PAIR_GROUP) + PAIR_GROUP + ITEM_UNROLL,), jnp.int32)


def _hside(part, bnd, idx, x1p, table, npair):
    T = x1p.shape[0]
    tt = TT_PEER
    return pl.pallas_call(
        functools.partial(_hside_kernel, part),
        out_shape=jax.ShapeDtypeStruct((npair, T), F32), grid=(T // tt,),
        in_specs=[*_peer_specs(npair, tt),
                  pl.BlockSpec((tt,) + x1p.shape[1:], lambda i: (i, 0, 0)),
                  pl.BlockSpec(memory_space=pltpu.VMEM)],
        out_specs=pl.BlockSpec((npair, tt), lambda i: (0, i)),
        scratch_shapes=[_items_scratch(npair, tt), pltpu.VMEM((tt * npair, LANES), F32)],
        compiler_params=_params(("arbitrary",)), name="hside",
    )(bnd, idx, x1p, table)


def _act_kernel(per_split, e_ref, g_ref, *refs):
    h_refs, a_refs = refs[:N_SPLIT], refs[N_SPLIT:]
    e = e_ref[...]
    in_part = [(e >= k * per_split) & (e < (k + 1) * per_split) for k in range(N_SPLIT)]
    h = h_refs[0][...]
    for k in range(1, N_SPLIT):
        h = jnp.where(in_part[k], h_refs[k][...], h)
    a = 0.5 * h * (1.0 + lax.erf(h * math.sqrt(0.5))) * g_ref[...]
    for k in range(N_SPLIT):
        a_refs[k][...] = jnp.where(in_part[k], a, 0.0).T


def _act(e, g, hs, per_split):
    npair, T = e.shape
    tt = min(1024, T)
    pair_major = pl.BlockSpec((npair, tt), lambda i: (0, i))
    token_major = pl.BlockSpec((tt, npair), lambda i: (i, 0))
    return pl.pallas_call(
        functools.partial(_act_kernel, per_split),
        out_shape=tuple(jax.ShapeDtypeStruct((T, npair), F32) for _ in range(N_SPLIT)),
        grid=(T // tt,), in_specs=[pair_major] * (2 + N_SPLIT),
        out_specs=tuple([token_major] * N_SPLIT),
        compiler_params=_params(("parallel",)), name="act",
    )(e, g, *hs)


def _yside_kernel(part, has_prev, npair, bnd_ref, idx_ref, a_ref, v_ref, *refs):
    y_ref, items_ref, part_ref = refs[-3], refs[-2], refs[-1]
    tt = y_ref.shape[0]
    groups = npair // PAIR_GROUP
    n_acc = 4
    nblk = _build_items(part, bnd_ref, items_ref, tt, groups)

    @pl.when(pl.program_id(0) == 0)
    def _():
        part_ref[...] = jnp.zeros(part_ref.shape, F32)

    def block(i, carry):
        for q in range(ITEM_UNROLL):
            it = items_ref[i * ITEM_UNROLL + q]
            base = it * PAIR_GROUP
            los, his = [None] * n_acc, [None] * n_acc
            for r in range(PAIR_GROUP):
                w = a_ref[base + r]
                v_lo, v_hi = _table_row(v_ref, idx_ref[base + r])
                k = r % n_acc
                los[k] = w * v_lo if los[k] is None else los[k] + w * v_lo
                his[k] = w * v_hi if his[k] is None else his[k] + w * v_hi
            lo = (los[0] + los[1]) + (los[2] + los[3])
            hi = (his[0] + his[1]) + (his[2] + his[3])
            part_ref[it] = jnp.concatenate([lo, hi], axis=0)
        return carry

    lax.fori_loop(0, nblk, block, 0)

    group_id = lax.broadcasted_iota(jnp.int32, (groups,) + part_ref.shape[1:], 0)

    def finish(t, carry):
        first, last = bnd_ref[SUBLANES * t + 2 * part], bnd_ref[SUBLANES * t + 2 * part + 1]
        sums = part_ref[pl.ds(pl.multiple_of(t * groups, groups), groups)]
        live = (group_id - first).astype(jnp.uint32) < (last - first).astype(jnp.uint32)
        y = jnp.sum(jnp.where(live, sums, 0.0), axis=0)
        if has_prev:
            y = y + refs[0][t]
        y_ref[t] = y
        return carry

    lax.fori_loop(0, tt, finish, 0, unroll=8)


def _yside(part, bnd, idx, a, table, prev, npair, packed):
    T = idx.shape[0] // npair
    tt = TT_PEER
    tile = pl.BlockSpec((tt,) + packed, lambda i: (i, 0, 0))
    bnd_spec, idx_spec = _peer_specs(npair, tt)
    in_specs = [bnd_spec, idx_spec, idx_spec, pl.BlockSpec(memory_space=pltpu.VMEM)]
    args = [bnd, idx, a, table]
    if prev is not None:
        in_specs.append(tile)
        args.append(prev)
    return pl.pallas_call(
        functools.partial(_yside_kernel, part, prev is not None, npair),
        out_shape=jax.ShapeDtypeStruct((T,) + packed, F32), grid=(T // tt,),
        in_specs=in_specs, out_specs=tile,
        scratch_shapes=[_items_scratch(npair, tt),
                        pltpu.VMEM((tt * (npair // PAIR_GROUP),) + packed, F32)],
        compiler_params=_params(("arbitrary",)), name="yside",
    )(*args)


def _ln2_kernel(alpha, x_ref, y_ref, g_ref, b_ref, o_ref):
    y = jnp.concatenate([y_ref[:, a, :] for a in range(y_ref.shape[1])], axis=-1)
    o_ref[...] = _layer_norm(alpha * x_ref[...] + y, g_ref[...], b_ref[...])


def _ln2(x1, y, g, b, alpha):
    T, D = x1.shape
    tm = 512
    tile = pl.BlockSpec((tm, D), lambda i: (i, 0))
    vec = pl.BlockSpec((1, D), lambda i: (0, 0))
    return pl.pallas_call(
        functools.partial(_ln2_kernel, alpha), out_shape=jax.ShapeDtypeStruct((T, D), F32),
        grid=(T // tm,),
        in_specs=[tile, pl.BlockSpec((tm,) + y.shape[1:], lambda i: (i, 0, 0)), vec, vec], out_specs=tile,
        compiler_params=_params(("parallel",)), name="ln2")(x1, y, g, b)


def _rope_tables(positions):
    T = positions.size
    half = ROT_DIM // 2
    inv_freq = ROPE_THETA ** (-jnp.arange(0, ROT_DIM, 2, dtype=F32) / ROT_DIM)
    ang = positions.reshape(T, 1).astype(F32) * inv_freq
    cos, sin = jnp.cos(ang), jnp.sin(ang)
    rest = HEAD_DIM - ROT_DIM
    zeros_h = jnp.zeros((T, half), F32)
    c = jnp.concatenate([cos, cos, jnp.ones((T, rest), F32)], axis=-1)
    s1 = jnp.concatenate([-sin, zeros_h, jnp.zeros((T, rest), F32)], axis=-1)
    s2 = jnp.concatenate([zeros_h, sin, jnp.zeros((T, rest), F32)], axis=-1)
    rep = LANES // HEAD_DIM
    return tuple(jnp.tile(t, (1, rep)) for t in (c, s1, s2))


def kernel(x, positions, w_in, b_in, conv_dw_w, conv_dw_b, conv_ln_g, conv_ln_b, w_conv_out, attn_sinks, w_mix_out, ln1_g, ln1_b, w_peer_q, sub_keys_1, sub_keys_2, expert_u, expert_v, ln2_g, ln2_b):
    B, S, D = x.shape
    T = B * S
    depth = w_in.shape[0]
    alpha = (2 * depth) ** 0.25
    dc = conv_dw_w.shape[-1]
    qw = N_HEADS * HEAD_DIM
    kvw = N_KV_HEADS * HEAD_DIM
    n_exp = expert_u.shape[1]
    per_split = n_exp // N_SPLIT
    packed = (D // LANES, LANES)
    c, s1, s2 = _rope_tables(positions)
    row = lambda v: v.reshape(1, -1)

    x2 = x.reshape(T, D)
    for l in range(depth):
        w = w_in[l].astype(BF16)
        b = b_in[l]
        o_q = 2 * dc
        o_k, o_v, o_ga = o_q + qw, o_q + qw + kvw, o_q + qw + 2 * kvw
        u = _glu(x2, w[:, :dc], w[:, dc:2 * dc], row(b[:dc]), row(b[dc:2 * dc]))
        qkv = _qkv(x2, w[:, o_q:o_ga], row(b[o_q:o_ga]), c, s1, s2, qw + kvw)
        gv = _gates(x2, w[:, o_ga:], row(b[o_ga:]))
        ya = _attention(qkv, attn_sinks[l], B, S)
        x1, x1b = _mix(u, conv_dw_w[l].reshape(-1, dc), row(conv_dw_b[l]), row(conv_ln_g[l]),
                       row(conv_ln_b[l]), w_conv_out[l].astype(BF16), gv, ya,
                       w_mix_out[l].astype(BF16), x2, row(ln1_g[l]), row(ln1_b[l]), alpha, B, S)
        outs = _topk(x1b, w_peer_q[l].astype(BF16), sub_keys_1[l].astype(BF16),
                     sub_keys_2[l].astype(BF16), per_split)
        e, g = outs[0], outs[1]
        npair = e.shape[0]
        bnd = outs[2].T.reshape(-1)
        idxs = [i.reshape(-1) for i in outs[3:]]
        x1p = x1b.reshape((T,) + packed)
        ut = _pack_table(expert_u[l], N_SPLIT)
        vt = _pack_table(expert_v[l], N_SPLIT)
        hs = [_hside(k, bnd, idxs[k], x1p, ut[k], npair) for k in range(N_SPLIT)]
        acts = _act(e, g, hs, per_split)
        y = None
        for k in range(N_SPLIT):
            y = _yside(k, bnd, idxs[k], acts[k].reshape(-1), vt[k], y, npair, packed)
        x2 = _ln2(x1, y, row(ln2_g[l]), row(ln2_b[l]), alpha)
    return x2.reshape(B, S, D)
```

```python
import functools
import math

import jax
import jax.numpy as jnp
from jax import lax
from jax.experimental import pallas as pl
from jax.experimental.pallas import tpu as pltpu

F32 = jnp.float32
BF16 = jnp.bfloat16

N_HEADS = 32
N_KV_HEADS = 4
GROUP = N_HEADS // N_KV_HEADS
HEAD_DIM = 64
WINDOW = 128
ROPE_THETA = 500000.0
ROT_DIM = HEAD_DIM // 4
ATTN_SCALE = 1.0 / math.sqrt(HEAD_DIM)
PEER_HEADS = 8
N_KEYS = 128
TOPK = 16
LN_EPS = 1e-5

LANES = 128
SUBLANES = 8
VMEM_LIMIT = 56 * 1024 * 1024

TM_PROJ = 1024
TN_PROJ = 512
TS_MIX = 256
HALO = 32
TT_TOPK = 256
TT_PEER = 128
N_SPLIT = 2
PAIR_GROUP = 16
H_ITEMS = 16
Y_ITEMS = 8
PACK_ROWS = 256


def _params(sem):
    return pltpu.CompilerParams(dimension_semantics=sem, vmem_limit_bytes=VMEM_LIMIT)


def _glu_kernel(x_ref, wa_ref, wb_ref, ba_ref, bb_ref, o_ref, xb_ref):
    @pl.when(pl.program_id(1) == 0)
    def _():
        xb_ref[...] = x_ref[...].astype(BF16)

    xb = xb_ref[...]
    za = jnp.dot(xb, wa_ref[...], preferred_element_type=F32) + ba_ref[...]
    zb = jnp.dot(xb, wb_ref[...], preferred_element_type=F32) + bb_ref[...]
    o_ref[...] = za * jax.nn.sigmoid(zb)


def _rope_block(z, c, s1, s2):
    half = ROT_DIM // 2
    return z * c + pltpu.roll(z, LANES - half, axis=1) * s1 + pltpu.roll(z, half, axis=1) * s2


def _qkv_kernel(n_rope_cols, x_ref, w_ref, b_ref, c_ref, s1_ref, s2_ref, o_ref, xb_ref):
    j = pl.program_id(1)

    @pl.when(j == 0)
    def _():
        xb_ref[...] = x_ref[...].astype(BF16)

    z = jnp.dot(xb_ref[...], w_ref[...], preferred_element_type=F32) + b_ref[...]
    c, s1, s2 = c_ref[...], s1_ref[...], s2_ref[...]
    tn = z.shape[1]
    for k in range(tn // LANES):
        sl = slice(LANES * k, LANES * (k + 1))
        rotary = j * tn + LANES * k < n_rope_cols
        o_ref[:, sl] = jnp.where(rotary, _rope_block(z[:, sl], c, s1, s2), z[:, sl]).astype(o_ref.dtype)


def _gates_kernel(x_ref, w_ref, b_ref, o_ref, xb_ref):
    @pl.when(pl.program_id(1) == 0)
    def _():
        xb_ref[...] = x_ref[...].astype(BF16)

    z = jnp.dot(xb_ref[...], w_ref[...], preferred_element_type=F32) + b_ref[...]
    o_ref[...] = jax.nn.sigmoid(z).astype(o_ref.dtype)


def _proj_specs(T, D, N, n_w, extra_specs=()):
    tm, tn = TM_PROJ, TN_PROJ
    assert T % tm == 0 and N % tn == 0, (T, N)
    grid = (T // tm, N // tn)
    in_specs = [pl.BlockSpec((tm, D), lambda i, j: (i, 0))]
    in_specs += [pl.BlockSpec((D, tn), lambda i, j: (0, j)) for _ in range(n_w)]
    in_specs += [pl.BlockSpec((1, tn), lambda i, j: (0, j)) for _ in range(n_w)]
    in_specs += list(extra_specs)
    out_spec = pl.BlockSpec((tm, tn), lambda i, j: (i, j))
    scratch = [pltpu.VMEM((tm, D), BF16)]
    return grid, in_specs, out_spec, scratch


def _glu(x2, wa, wb, ba, bb):
    T, D = x2.shape
    N = wa.shape[1]
    grid, in_specs, out_spec, scratch = _proj_specs(T, D, N, 2)
    return pl.pallas_call(
        _glu_kernel, out_shape=jax.ShapeDtypeStruct((T, N), F32), grid=grid,
        in_specs=in_specs, out_specs=out_spec, scratch_shapes=scratch,
        compiler_params=_params(("parallel", "arbitrary")), name="glu")(x2, wa, wb, ba, bb)


def _qkv(x2, w, b, c, s1, s2, n_rope_cols):
    T, D = x2.shape
    N = w.shape[1]
    tab = pl.BlockSpec((TM_PROJ, LANES), lambda i, j: (i, 0))
    grid, in_specs, out_spec, scratch = _proj_specs(T, D, N, 1, (tab, tab, tab))
    return pl.pallas_call(
        functools.partial(_qkv_kernel, n_rope_cols),
        out_shape=jax.ShapeDtypeStruct((T, N), BF16), grid=grid,
        in_specs=in_specs, out_specs=out_spec, scratch_shapes=scratch,
        compiler_params=_params(("parallel", "arbitrary")), name="qkv")(x2, w, b, c, s1, s2)


def _gates(x2, w, b):
    T, D = x2.shape
    N = w.shape[1]
    grid, in_specs, out_spec, scratch = _proj_specs(T, D, N, 1)
    return pl.pallas_call(
        _gates_kernel, out_shape=jax.ShapeDtypeStruct((T, N), BF16), grid=grid,
        in_specs=in_specs, out_specs=out_spec, scratch_shapes=scratch,
        compiler_params=_params(("parallel", "arbitrary")), name="gates")(x2, w, b)


def _attn_kernel(sink_ref, q_ref, kc_ref, kp_ref, vc_ref, vp_ref, o_ref):
    n = pl.program_id(1)
    blk = WINDOW
    lane = lax.broadcasted_iota(jnp.int32, (blk, LANES), 1)
    lo = lane < HEAD_DIM
    lane2 = lax.broadcasted_iota(jnp.int32, (2 * blk, LANES), 1)
    lo2 = lane2 < HEAD_DIM
    qi = lax.broadcasted_iota(jnp.int32, (blk, 2 * blk), 0)
    sj = lax.broadcasted_iota(jnp.int32, (blk, 2 * blk), 1)
    valid = (sj > qi) & (sj <= qi + blk) & (sj >= blk * (1 - n))
    neg = jnp.finfo(F32).min
    kk = jnp.concatenate([kp_ref[...], kc_ref[...]], axis=0).astype(F32)
    vv = jnp.concatenate([vp_ref[...], vc_ref[...]], axis=0).astype(F32)
    heads_per_tile = LANES // HEAD_DIM
    tiles_per_group = GROUP // heads_per_tile
    for g in range(N_KV_HEADS):
        tsl = slice(LANES * (g // heads_per_tile), LANES * (g // heads_per_tile + 1))

        def both_halves(t):
            r = pltpu.roll(t, HEAD_DIM, axis=1)
            return (jnp.where(lo2, t, r) if g % heads_per_tile == 0 else jnp.where(lo2, r, t)).astype(BF16)

        kd = both_halves(kk[:, tsl])
        vd = both_halves(vv[:, tsl])
        qs = []
        for c in range(tiles_per_group):
            col = LANES * (tiles_per_group * g + c)
            qt = q_ref[:, col:col + LANES]
            zero = jnp.zeros_like(qt)
            qs.append(jnp.where(lo, qt, zero))
            qs.append(jnp.where(lo, zero, qt))
        q8 = jnp.concatenate(qs, axis=0)
        s = lax.dot_general(q8, kd, (((1,), (1,)), ((), ())), preferred_element_type=F32) * ATTN_SCALE
        s3 = s.reshape(GROUP, blk, 2 * blk)
        ps = []
        half = GROUP // 2
        for h0 in range(0, GROUP, half):
            sh = jnp.where(valid[None], s3[h0:h0 + half], neg)
            sink = sink_ref[GROUP * g + h0:GROUP * g + h0 + half]
            m = jnp.maximum(jnp.max(sh, axis=-1, keepdims=True), sink)
            p = jnp.exp(sh - m)
            denom = jnp.sum(p, axis=-1, keepdims=True) + jnp.exp(sink - m)
            ps.append((p * (1.0 / denom)).astype(BF16))
        p = jnp.concatenate(ps, axis=0).reshape(GROUP * blk, 2 * blk)
        o = jnp.dot(p, vd, preferred_element_type=F32)
        for c in range(tiles_per_group):
            oa = o[blk * (2 * c):blk * (2 * c + 1)]
            ob = o[blk * (2 * c + 1):blk * (2 * c + 2)]
            col = LANES * (tiles_per_group * g + c)
            o_ref[:, col:col + LANES] = jnp.where(lo, oa, ob).astype(o_ref.dtype)


def _attention(qkv, sinks, B, S):
    T = B * S
    blk = WINDOW
    nb = S // blk
    qw = N_HEADS * HEAD_DIM
    kvw = N_KV_HEADS * HEAD_DIM
    k_col = qw // kvw
    v_col = k_col + 1
    row = lambda b, n: b * nb + n
    prev = lambda b, n: jnp.maximum(b * nb + n - 1, 0)
    in_specs = [
        pl.BlockSpec((N_HEADS, 1, 1), lambda b, n: (0, 0, 0)),
        pl.BlockSpec((blk, qw), lambda b, n: (row(b, n), 0)),
        pl.BlockSpec((blk, kvw), lambda b, n: (row(b, n), k_col)),
        pl.BlockSpec((blk, kvw), lambda b, n: (prev(b, n), k_col)),
        pl.BlockSpec((blk, kvw), lambda b, n: (row(b, n), v_col)),
        pl.BlockSpec((blk, kvw), lambda b, n: (prev(b, n), v_col)),
    ]
    return pl.pallas_call(
        _attn_kernel, out_shape=jax.ShapeDtypeStruct((T, qw), BF16), grid=(B, nb),
        in_specs=in_specs, out_specs=pl.BlockSpec((blk, qw), lambda b, n: (row(b, n), 0)),
        compiler_params=_params(("parallel", "arbitrary")), name="attn",
    )(sinks.astype(F32).reshape(N_HEADS, 1, 1), qkv, qkv, qkv, qkv, qkv)


def _layer_norm(x, g, b):
    mu = jnp.mean(x, axis=-1, keepdims=True)
    xc = x - mu
    var = jnp.mean(xc * xc, axis=-1, keepdims=True)
    return xc * lax.rsqrt(var + LN_EPS) * g + b


def _mix_kernel(alpha, u_ref, uh_ref, cw_ref, cb_ref, cg_ref, cbeta_ref, wco_ref, sa_ref, sb_ref,
                ya_ref, wmo_ref, x_ref, g1_ref, b1_ref, x1_ref, x1b_ref, win_ref, conv_ref, shift_ref):
    s = pl.program_id(1)
    ts, dc = u_ref.shape
    width = cw_ref.shape[0]
    win_ref[0:HALO, :] = jnp.where(s == 0, 0.0, uh_ref[...])
    win_ref[HALO:HALO + ts, :] = u_ref[...]
    first = HALO - (width - 1)
    span = ts + HALO - SUBLANES
    for c in range(dc // LANES):
        cs = slice(LANES * c, LANES * (c + 1))
        for r in range(1, SUBLANES):
            shift_ref[r, 0:span, :] = win_ref[r:r + span, cs]
        acc = jnp.broadcast_to(cb_ref[:, cs], (ts, LANES))
        for j in range(width):
            r, a = (first + j) % SUBLANES, (first + j) // SUBLANES * SUBLANES
            rows = win_ref[a:a + ts, cs] if r == 0 else shift_ref[r, a:a + ts, :]
            acc = acc + cw_ref[j:j + 1, cs] * rows
        conv_ref[:, cs] = acc
    h = _layer_norm(conv_ref[...], cg_ref[...], cbeta_ref[...])
    h = h * jax.nn.sigmoid(h)
    yc = jnp.dot(h.astype(BF16), wco_ref[...], preferred_element_type=F32)
    m = sa_ref[...].astype(F32) * yc + sb_ref[...].astype(F32) * ya_ref[...].astype(F32)
    mix = jnp.dot(m.astype(BF16), wmo_ref[...], preferred_element_type=F32)
    x1 = _layer_norm(alpha * x_ref[...] + mix, g1_ref[...], b1_ref[...])
    x1_ref[...] = x1
    x1b_ref[...] = x1.astype(BF16)


def _mix(u, cw, cb, cg, cbeta, wco, gv, ya, wmo, x2, g1, b1, alpha, B, S):
    T, D = x2.shape
    dc = u.shape[1]
    ts = TS_MIX
    ns = S // ts
    row = lambda b, s: b * ns + s
    halo = lambda b, s: jnp.maximum((b * S + s * ts) // HALO - 1, 0)
    const = lambda shape: pl.BlockSpec(shape, lambda b, s: (0, 0))
    in_specs = [
        pl.BlockSpec((ts, dc), lambda b, s: (row(b, s), 0)),
        pl.BlockSpec((HALO, dc), lambda b, s: (halo(b, s), 0)),
        const(cw.shape), const((1, dc)), const((1, dc)), const((1, dc)),
        const(wco.shape),
        pl.BlockSpec((ts, D), lambda b, s: (row(b, s), 0)),
        pl.BlockSpec((ts, D), lambda b, s: (row(b, s), 1)),
        pl.BlockSpec((ts, D), lambda b, s: (row(b, s), 0)),
        const(wmo.shape),
        pl.BlockSpec((ts, D), lambda b, s: (row(b, s), 0)),
        const((1, D)), const((1, D)),
    ]
    out_spec = pl.BlockSpec((ts, D), lambda b, s: (row(b, s), 0))
    return pl.pallas_call(
        functools.partial(_mix_kernel, alpha),
        out_shape=(jax.ShapeDtypeStruct((T, D), F32), jax.ShapeDtypeStruct((T, D), BF16)),
        grid=(B, ns), in_specs=in_specs, out_specs=(out_spec, out_spec),
        scratch_shapes=[pltpu.VMEM((HALO + ts, dc), F32), pltpu.VMEM((ts, dc), F32),
                        pltpu.VMEM((SUBLANES, HALO + ts, LANES), F32)],
        compiler_params=_params(("parallel", "arbitrary")), name="mix",
    )(u, u, cw, cb, cg, cbeta, wco, gv, gv, ya, wmo, x2, g1, b1)


def _topk_axis0(s, k, payload=None, rank=None):
    iota = lax.broadcasted_iota(jnp.int32, s.shape, 0) if rank is None else rank
    big = jnp.iinfo(jnp.int32).max
    vals, outs = [], []
    for _ in range(k):
        m = jnp.max(s, axis=0, keepdims=True)
        idx = jnp.min(jnp.where(s == m, iota, big), axis=0, keepdims=True)
        sel = iota == idx
        vals.append(m)
        if payload is None:
            outs.append(idx)
        else:
            outs.append(jnp.max(jnp.where(sel, payload, -1), axis=0, keepdims=True))
        s = jnp.where(sel, -jnp.inf, s)
    return jnp.concatenate(vals, axis=0), jnp.concatenate(outs, axis=0)


def _candidate_grid(v1, i1, v2, i2):
    k, t = v1.shape
    sub = lax.broadcasted_iota(jnp.int32, (SUBLANES, t), 0)
    vals, ids, flats = [], [], []
    for a in range(k // 2):
        nb = k // (a + 1)
        for b0 in range(0, nb, SUBLANES):
            live = sub < (nb - b0)
            vals.append(jnp.where(live, v1[a:a + 1] + v2[b0:b0 + SUBLANES], -jnp.inf))
            ids.append(i1[a:a + 1] * N_KEYS + i2[b0:b0 + SUBLANES])
            flats.append(a * k + b0 + sub)
    for a0 in range(k // 2, k, SUBLANES):
        vals.append(v1[a0:a0 + SUBLANES] + v2[0:1])
        ids.append(i1[a0:a0 + SUBLANES] * N_KEYS + i2[0:1])
        flats.append((a0 + sub) * k)
    return jnp.concatenate(vals, axis=0), jnp.concatenate(ids, axis=0), jnp.concatenate(flats, axis=0)


def _partner_rows(x, j):
    n = x.shape[0]
    if j >= SUBLANES:
        return jnp.concatenate([x[(b ^ 1) * j:((b ^ 1) + 1) * j] for b in range(n // j)], axis=0)
    sub = lax.broadcasted_iota(jnp.int32, (SUBLANES, x.shape[1]), 0)
    low = (sub & j) == 0
    out = []
    for r in range(n // SUBLANES):
        slab = x[SUBLANES * r:SUBLANES * (r + 1)]
        out.append(jnp.where(low, pltpu.roll(slab, SUBLANES - j, axis=0), pltpu.roll(slab, j, axis=0)))
    return jnp.concatenate(out, axis=0)


def _sort_axis0(key, val):
    n = key.shape[0]
    row = lax.broadcasted_iota(jnp.int32, key.shape, 0)
    k = 2
    while k <= n:
        j = k // 2
        while j >= 1:
            pk, pv = _partner_rows(key, j), _partner_rows(val, j)
            sign = 1 - 2 * (((row & j) != 0).astype(jnp.int32) ^ ((row & k) != 0).astype(jnp.int32))
            take = (pk - key) * sign < 0
            key = jnp.where(take, pk, key)
            val = jnp.where(take, pv, val)
            j //= 2
        k *= 2
    return key, val


def _topk_kernel(per_split, x_ref, wq_ref, k1_ref, k2_ref, e_ref, g_ref, bnd_ref, *idx_refs):
    q = jnp.dot(x_ref[...], wq_ref[...], preferred_element_type=F32).astype(BF16)
    nt = (((1,), (1,)), ((), ()))
    half = k1_ref.shape[1]
    es, gs = [], []
    for h in range(PEER_HEADS):
        q1 = q[:, 2 * half * h:2 * half * h + half]
        q2 = q[:, 2 * half * h + half:2 * half * (h + 1)]
        s1 = lax.dot_general(k1_ref[...], q1, nt, preferred_element_type=F32)
        s2 = lax.dot_general(k2_ref[...], q2, nt, preferred_element_type=F32)
        v1, i1 = _topk_axis0(s1, TOPK)
        v2, i2 = _topk_axis0(s2, TOPK)
        cand, cidx, flat = _candidate_grid(v1, i1, v2, i2)
        best, eidx = _topk_axis0(cand, TOPK, payload=cidx, rank=flat)
        w = jnp.exp(best - best[0:1])
        es.append(eidx)
        gs.append(w / jnp.sum(w, axis=0, keepdims=True))
    e, g = _sort_axis0(jnp.concatenate(es, axis=0), jnp.concatenate(gs, axis=0))
    e_ref[...] = e
    g_ref[...] = g
    shift = PAIR_GROUP.bit_length() - 1
    bounds = []
    for part, idx_ref in enumerate(idx_refs):
        idx_ref[...] = (jnp.clip(e - part * per_split, 0, per_split - 1) * SUBLANES).T
        below = jnp.sum((e < part * per_split).astype(jnp.int32), axis=0, keepdims=True)
        upto = jnp.sum((e < (part + 1) * per_split).astype(jnp.int32), axis=0, keepdims=True)
        bounds += [below >> shift, (upto + (PAIR_GROUP - 1)) >> shift]
    pad = bnd_ref.shape[0] - len(bounds)
    bnd_ref[...] = jnp.concatenate(bounds + [jnp.zeros_like(bounds[0])] * pad, axis=0)


def _topk(x1b, wq, k1, k2, per_split):
    T, D = x1b.shape
    tt = TT_TOPK
    npair = PEER_HEADS * TOPK
    const = lambda shape: pl.BlockSpec(shape, lambda i: (0, 0))
    pair_major = pl.BlockSpec((npair, tt), lambda i: (0, i))
    token_major = pl.BlockSpec((tt, npair), lambda i: (i, 0))
    return pl.pallas_call(
        functools.partial(_topk_kernel, per_split),
        out_shape=(jax.ShapeDtypeStruct((npair, T), jnp.int32), jax.ShapeDtypeStruct((npair, T), F32),
                   jax.ShapeDtypeStruct((SUBLANES, T), jnp.int32))
        + tuple(jax.ShapeDtypeStruct((T, npair), jnp.int32) for _ in range(N_SPLIT)),
        grid=(T // tt,),
        in_specs=[pl.BlockSpec((tt, D), lambda i: (i, 0)), const(wq.shape), const(k1.shape), const(k2.shape)],
        out_specs=(pair_major, pair_major, pl.BlockSpec((SUBLANES, tt), lambda i: (0, i)))
        + tuple(token_major for _ in range(N_SPLIT)),
        compiler_params=_params(("parallel",)), name="topk",
    )(x1b, wq, k1, k2)


def _build_items(part, bnd_ref, items_ref, tt, groups, per_trip):
    def per_token(t, n):
        first, last = bnd_ref[SUBLANES * t + 2 * part], bnd_ref[SUBLANES * t + 2 * part + 1]
        for g in range(groups):
            items_ref[n + g] = t * groups + first + g
        return n + (last - first)

    n = lax.fori_loop(0, tt, per_token, 0)
    final = items_ref[jnp.maximum(n - 1, 0)]
    for q in range(per_trip - 1):
        items_ref[n + q] = final
    return lax.shift_right_logical(n + (per_trip - 1), per_trip.bit_length() - 1)


def _pack_table(table, pieces):
    n, d = table.shape
    eb = min(PACK_ROWS, n)
    assert d == 2 * SUBLANES * LANES and n % eb == 0, (n, d)
    words = pl.pallas_call(
        _pack_kernel, out_shape=jax.ShapeDtypeStruct((n, SUBLANES, LANES), jnp.uint32), grid=(n // eb,),
        in_specs=[pl.BlockSpec((eb, d), lambda i: (i, 0))],
        out_specs=pl.BlockSpec((eb, SUBLANES, LANES), lambda i: (i, 0, 0)),
        compiler_params=_params(("parallel",)), name="pack")(table)
    return words.reshape(pieces, (n // pieces) * SUBLANES, LANES)


def _pack_kernel(t_ref, o_ref):
    x = t_ref[...]
    half = x.shape[1] // 2
    bits = pltpu.bitcast(x.astype(BF16).astype(F32), jnp.uint32)
    words = (bits[:, :half] >> 16) | bits[:, half:]
    for s in range(SUBLANES):
        o_ref[:, s, :] = words[:, LANES * s:LANES * (s + 1)]


def _table_row(tab_ref, row8):
    words = tab_ref[pl.ds(pl.multiple_of(row8, SUBLANES), SUBLANES), :]
    lo = pltpu.bitcast(words << 16, F32)
    hi = pltpu.bitcast(words & jnp.uint32(0xFFFF0000), F32)
    return lo, hi


def _rows_to_sublanes(parts):
    sub = lax.broadcasted_iota(jnp.int32, (SUBLANES, LANES), 0)
    bits = SUBLANES.bit_length() - 1
    parts = [parts[int(format(i, f"0{bits}b")[::-1], 2)] for i in range(SUBLANES)]
    step = SUBLANES // 2
    while len(parts) > 1:
        low = (sub & step) == 0
        nxt = []
        for a, b in zip(parts[0::2], parts[1::2]):
            if 2 * step == SUBLANES:
                nxt.append(jnp.where(low, a, b) + pltpu.roll(jnp.where(low, b, a), step, axis=0))
            else:
                t = jnp.where(low, a, pltpu.roll(b, step, axis=0))
                u = jnp.where(low, pltpu.roll(a, SUBLANES - step, axis=0), b)
                nxt.append(t + u)
        parts = nxt
        step //= 2
    return parts[0]


def _hside_kernel(part, bnd_ref, idx_ref, x_ref, u_ref, h_ref, items_ref, sums_ref):
    npair, tt = h_ref.shape
    groups = npair // PAIR_GROUP
    gshift = groups.bit_length() - 1
    nblk = _build_items(part, bnd_ref, items_ref, tt, groups, H_ITEMS)

    @pl.when(pl.program_id(0) == 0)
    def _():
        sums_ref[...] = jnp.zeros(sums_ref.shape, F32)

    def block(i, carry):
        for q in range(H_ITEMS):
            it = items_ref[i * H_ITEMS + q]
            t = lax.shift_right_logical(it, gshift)
            base = it * PAIR_GROUP
            xf = x_ref[t].astype(F32)
            x_lo, x_hi = xf[0:SUBLANES], xf[SUBLANES:2 * SUBLANES]
            for sub in range(PAIR_GROUP // SUBLANES):
                parts = []
                for r in range(SUBLANES):
                    u_lo, u_hi = _table_row(u_ref, idx_ref[base + sub * SUBLANES + r])
                    parts.append(x_lo * u_lo + x_hi * u_hi)
                row0 = pl.multiple_of(base + sub * SUBLANES, SUBLANES)
                sums_ref[pl.ds(row0, SUBLANES), :] = _rows_to_sublanes(parts)
        return carry

    lax.fori_loop(0, nblk, block, 0)

    lane = lax.broadcasted_iota(jnp.int32, (npair, tt), 1)
    h_ref[...] = jnp.zeros(h_ref.shape, F32)

    def finish(t, carry):
        rows = sums_ref[pl.ds(pl.multiple_of(t * npair, npair), npair), :]
        h_ref[...] = jnp.where(lane == t, jnp.sum(rows, axis=1, keepdims=True), h_ref[...])
        return carry

    lax.fori_loop(0, tt, finish, 0, unroll=32)


def _peer_specs(npair, tt):
    bnd = pl.BlockSpec((tt * SUBLANES,), lambda i: (i,), memory_space=pltpu.SMEM)
    idx = pl.BlockSpec((tt * npair,), lambda i: (i,), memory_space=pltpu.SMEM)
    return bnd, idx


def _items_scratch(npair, tt):
    return pltpu.SMEM((tt * (npair // PAIR_GROUP) + PAIR_GROUP + max(H_ITEMS, Y_ITEMS),), jnp.int32)


def _hside(part, bnd, idx, x1p, table, npair):
    T = x1p.shape[0]
    tt = TT_PEER
    return pl.pallas_call(
        functools.partial(_hside_kernel, part),
        out_shape=jax.ShapeDtypeStruct((npair, T), F32), grid=(T // tt,),
        in_specs=[*_peer_specs(npair, tt),
                  pl.BlockSpec((tt,) + x1p.shape[1:], lambda i: (i, 0, 0)),
                  pl.BlockSpec(memory_space=pltpu.VMEM)],
        out_specs=pl.BlockSpec((npair, tt), lambda i: (0, i)),
        scratch_shapes=[_items_scratch(npair, tt), pltpu.VMEM((tt * npair, LANES), F32)],
        compiler_params=_params(("arbitrary",)), name="hside",
    )(bnd, idx, x1p, table)


def _act_kernel(per_split, e_ref, g_ref, *refs):
    h_refs, a_refs = refs[:N_SPLIT], refs[N_SPLIT:]
    e = e_ref[...]
    in_part = [(e >= k * per_split) & (e < (k + 1) * per_split) for k in range(N_SPLIT)]
    h = h_refs[0][...]
    for k in range(1, N_SPLIT):
        h = jnp.where(in_part[k], h_refs[k][...], h)
    a = 0.5 * h * (1.0 + lax.erf(h * math.sqrt(0.5))) * g_ref[...]
    for k in range(N_SPLIT):
        a_refs[k][...] = jnp.where(in_part[k], a, 0.0).T


def _act(e, g, hs, per_split):
    npair, T = e.shape
    tt = min(1024, T)
    pair_major = pl.BlockSpec((npair, tt), lambda i: (0, i))
    token_major = pl.BlockSpec((tt, npair), lambda i: (i, 0))
    return pl.pallas_call(
        functools.partial(_act_kernel, per_split),
        out_shape=tuple(jax.ShapeDtypeStruct((T, npair), F32) for _ in range(N_SPLIT)),
        grid=(T // tt,), in_specs=[pair_major] * (2 + N_SPLIT),
        out_specs=tuple([token_major] * N_SPLIT),
        compiler_params=_params(("parallel",)), name="act",
    )(e, g, *hs)


def _yside_kernel(part, has_prev, npair, bnd_ref, idx_ref, a_ref, v_ref, *refs):
    y_ref, items_ref, part_ref = refs[-3], refs[-2], refs[-1]
    tt = y_ref.shape[0]
    groups = npair // PAIR_GROUP
    n_acc = 4
    nblk = _build_items(part, bnd_ref, items_ref, tt, groups, Y_ITEMS)

    @pl.when(pl.program_id(0) == 0)
    def _():
        part_ref[...] = jnp.zeros(part_ref.shape, F32)

    def block(i, carry):
        for q in range(Y_ITEMS):
            it = items_ref[i * Y_ITEMS + q]
            base = it * PAIR_GROUP
            los, his = [None] * n_acc, [None] * n_acc
            for r in range(PAIR_GROUP):
                w = a_ref[base + r]
                v_lo, v_hi = _table_row(v_ref, idx_ref[base + r])
                k = r % n_acc
                los[k] = w * v_lo if los[k] is None else los[k] + w * v_lo
                his[k] = w * v_hi if his[k] is None else his[k] + w * v_hi
            lo = (los[0] + los[1]) + (los[2] + los[3])
            hi = (his[0] + his[1]) + (his[2] + his[3])
            part_ref[it] = jnp.concatenate([lo, hi], axis=0)
        return carry

    lax.fori_loop(0, nblk, block, 0)

    group_id = lax.broadcasted_iota(jnp.int32, (groups,) + part_ref.shape[1:], 0)

    def finish(t, carry):
        first, last = bnd_ref[SUBLANES * t + 2 * part], bnd_ref[SUBLANES * t + 2 * part + 1]
        sums = part_ref[pl.ds(pl.multiple_of(t * groups, groups), groups)]
        live = (group_id - first).astype(jnp.uint32) < (last - first).astype(jnp.uint32)
        y = jnp.sum(jnp.where(live, sums, 0.0), axis=0)
        if has_prev:
            y = y + refs[0][t]
        y_ref[t] = y
        return carry

    lax.fori_loop(0, tt, finish, 0, unroll=8)


def _yside(part, bnd, idx, a, table, prev, npair, packed):
    T = idx.shape[0] // npair
    tt = TT_PEER
    tile = pl.BlockSpec((tt,) + packed, lambda i: (i, 0, 0))
    bnd_spec, idx_spec = _peer_specs(npair, tt)
    in_specs = [bnd_spec, idx_spec, idx_spec, pl.BlockSpec(memory_space=pltpu.VMEM)]
    args = [bnd, idx, a, table]
    if prev is not None:
        in_specs.append(tile)
        args.append(prev)
    return pl.pallas_call(
        functools.partial(_yside_kernel, part, prev is not None, npair),
        out_shape=jax.ShapeDtypeStruct((T,) + packed, F32), grid=(T // tt,),
        in_specs=in_specs, out_specs=tile,
        scratch_shapes=[_items_scratch(npair, tt),
                        pltpu.VMEM((tt * (npair // PAIR_GROUP),) + packed, F32)],
        compiler_params=_params(("arbitrary",)), name="yside",
    )(*args)


def _ln2_kernel(alpha, x_ref, y_ref, g_ref, b_ref, o_ref):
    y = jnp.concatenate([y_ref[:, a, :] for a in range(y_ref.shape[1])], axis=-1)
    o_ref[...] = _layer_norm(alpha * x_ref[...] + y, g_ref[...], b_ref[...])


def _ln2(x1, y, g, b, alpha):
    T, D = x1.shape
    tm = 512
    tile = pl.BlockSpec((tm, D), lambda i: (i, 0))
    vec = pl.BlockSpec((1, D), lambda i: (0, 0))
    return pl.pallas_call(
        functools.partial(_ln2_kernel, alpha), out_shape=jax.ShapeDtypeStruct((T, D), F32),
        grid=(T // tm,),
        in_specs=[tile, pl.BlockSpec((tm,) + y.shape[1:], lambda i: (i, 0, 0)), vec, vec], out_specs=tile,
        compiler_params=_params(("parallel",)), name="ln2")(x1, y, g, b)


def _rope_tables(positions):
    T = positions.size
    half = ROT_DIM // 2
    inv_freq = ROPE_THETA ** (-jnp.arange(0, ROT_DIM, 2, dtype=F32) / ROT_DIM)
    ang = positions.reshape(T, 1).astype(F32) * inv_freq
    cos, sin = jnp.cos(ang), jnp.sin(ang)
    rest = HEAD_DIM - ROT_DIM
    zeros_h = jnp.zeros((T, half), F32)
    c = jnp.concatenate([cos, cos, jnp.ones((T, rest), F32)], axis=-1)
    s1 = jnp.concatenate([-sin, zeros_h, jnp.zeros((T, rest), F32)], axis=-1)
    s2 = jnp.concatenate([zeros_h, sin, jnp.zeros((T, rest), F32)], axis=-1)
    rep = LANES // HEAD_DIM
    return tuple(jnp.tile(t, (1, rep)) for t in (c, s1, s2))


def kernel(x, positions, w_in, b_in, conv_dw_w, conv_dw_b, conv_ln_g, conv_ln_b, w_conv_out, attn_sinks, w_mix_out, ln1_g, ln1_b, w_peer_q, sub_keys_1, sub_keys_2, expert_u, expert_v, ln2_g, ln2_b):
    B, S, D = x.shape
    T = B * S
    depth = w_in.shape[0]
    alpha = (2 * depth) ** 0.25
    dc = conv_dw_w.shape[-1]
    qw = N_HEADS * HEAD_DIM
    kvw = N_KV_HEADS * HEAD_DIM
    n_exp = expert_u.shape[1]
    per_split = n_exp // N_SPLIT
    packed = (D // LANES, LANES)
    c, s1, s2 = _rope_tables(positions)
    row = lambda v: v.reshape(1, -1)

    x2 = x.reshape(T, D)
    for l in range(depth):
        w = w_in[l].astype(BF16)
        b = b_in[l]
        o_q = 2 * dc
        o_k, o_v, o_ga = o_q + qw, o_q + qw + kvw, o_q + qw + 2 * kvw
        u = _glu(x2, w[:, :dc], w[:, dc:2 * dc], row(b[:dc]), row(b[dc:2 * dc]))
        qkv = _qkv(x2, w[:, o_q:o_ga], row(b[o_q:o_ga]), c, s1, s2, qw + kvw)
        gv = _gates(x2, w[:, o_ga:], row(b[o_ga:]))
        ya = _attention(qkv, attn_sinks[l], B, S)
        x1, x1b = _mix(u, conv_dw_w[l].reshape(-1, dc), row(conv_dw_b[l]), row(conv_ln_g[l]),
                       row(conv_ln_b[l]), w_conv_out[l].astype(BF16), gv, ya,
                       w_mix_out[l].astype(BF16), x2, row(ln1_g[l]), row(ln1_b[l]), alpha, B, S)
        outs = _topk(x1b, w_peer_q[l].astype(BF16), sub_keys_1[l].astype(BF16),
                     sub_keys_2[l].astype(BF16), per_split)
        e, g = outs[0], outs[1]
        npair = e.shape[0]
        bnd = outs[2].T.reshape(-1)
        idxs = [i.reshape(-1) for i in outs[3:]]
        x1p = x1b.reshape((T,) + packed)
        ut = _pack_table(expert_u[l], N_SPLIT)
        vt = _pack_table(expert_v[l], N_SPLIT)
        hs = [_hside(k, bnd, idxs[k], x1p, ut[k], npair) for k in range(N_SPLIT)]
        acts = _act(e, g, hs, per_split)
        y = None
        for k in range(N_SPLIT):
            y = _yside(k, bnd, idxs[k], acts[k].reshape(-1), vt[k], y, npair, packed)
        x2 = _ln2(x1, y, row(ln2_g[l]), row(ln2_b[l]), alpha)
    return x2.reshape(B, S, D)
```
